```python
import jax, jax.numpy as jnp
from jax import lax
import numpy as np

D_MODEL = 1024
BATCH = 8
SEQ = 4096
DEPTH = 2

ATT_HEADS = 8
ATT_HEAD_DIM = 64
D_ATT = ATT_HEADS * ATT_HEAD_DIM
SSD_HEADS = 8
SSD_HEAD_DIM = 64
D_SSD = SSD_HEADS * SSD_HEAD_DIM
SSD_GROUPS = 2
SSD_STATE = 128
SSD_CONV = 4
SSD_CHUNK = 128
D_MIX = D_ATT + D_SSD
D_CONV_CH = D_SSD + 2 * SSD_GROUPS * SSD_STATE
D_IN = 3 * D_ATT + ATT_HEADS + D_SSD + D_CONV_CH + SSD_HEADS
Q_BLOCK = 128
D_FF = 2816
FFN_CONV = 3
N_MOD = 6
EPS = 1e-6

kernel_name = "fox_ssd_parallel_hybrid_block"


def _rmsnorm(x, g):
    xf = x.astype(jnp.float32)
    y = xf * lax.rsqrt(jnp.mean(xf * xf, axis=-1, keepdims=True) + EPS)
    return (y * g.astype(jnp.float32)).astype(x.dtype)


def _causal_dwconv(u, w, b):
    k, ch = w.shape
    out = lax.conv_general_dilated(
        u, w[:, None, :].astype(u.dtype), window_strides=(1,), padding=[(k - 1, 0)],
        dimension_numbers=('NWC', 'WIO', 'NWC'), feature_group_count=ch)
    return out + b.astype(u.dtype)


def _forgetting_attention(q, k, v, log_f):
    bsz, s, h, d = q.shape
    nb = s // Q_BLOCK
    f_cum = jnp.cumsum(log_f, axis=1).transpose(0, 2, 1)
    q_blocks = q.reshape(bsz, nb, Q_BLOCK, h, d).transpose(1, 0, 2, 3, 4)
    fq_blocks = f_cum.reshape(bsz, h, nb, Q_BLOCK).transpose(2, 0, 1, 3)
    k_pos = jnp.arange(s)
    scale = d ** -0.5

    def one_block(args):
        i, q_i, fq_i = args
        logits = jnp.einsum('bqhd,bkhd->bhqk', q_i, k).astype(jnp.float32) * scale
        logits = logits + fq_i[..., :, None] - f_cum[:, :, None, :]
        q_pos = i * Q_BLOCK + jnp.arange(Q_BLOCK)
        mask = k_pos[None, :] <= q_pos[:, None]
        logits = jnp.where(mask, logits, -jnp.inf)
        p = jax.nn.softmax(logits, axis=-1)
        return jnp.einsum('bhqk,bkhd->bqhd', p.astype(v.dtype), v)

    out = lax.map(one_block, (jnp.arange(nb), q_blocks, fq_blocks))
    return out.transpose(1, 0, 2, 3, 4).reshape(bsz, s, h * d)


def _ssd_chunked(xs, dt, a_neg, b_in, c_in, d_skip):
    bsz, s, h, p = xs.shape
    g, n = b_in.shape[-2:]
    r = h // g
    nc = s // SSD_CHUNK
    L = SSD_CHUNK
    dtype = xs.dtype
    xdt = (xs * dt[..., None].astype(dtype)).reshape(bsz, nc, L, g, r, p)
    a = (dt * a_neg).reshape(bsz, nc, L, g, r).transpose(0, 3, 4, 1, 2)
    a_cs = jnp.cumsum(a, axis=-1)
    bc = b_in.reshape(bsz, nc, L, g, n)
    cc = c_in.reshape(bsz, nc, L, g, n)
    causal = jnp.tril(jnp.ones((L, L), dtype=bool))
    seg = a_cs[..., :, None] - a_cs[..., None, :]
    decay_in = jnp.exp(jnp.where(causal, seg, -jnp.inf)).astype(dtype)
    cb = jnp.einsum('bclgn,bcsgn->bgcls', cc, bc)
    y_diag = jnp.einsum('bgrcls,bcsgrp->bclgrp', cb[:, :, None] * decay_in, xdt)
    decay_to_end = jnp.exp(a_cs[..., -1:] - a_cs).astype(dtype).transpose(0, 3, 4, 1, 2)
    states = jnp.einsum('bclgn,bclgrp->bcgrpn', bc, xdt * decay_to_end[..., None])
    chunk_decay = jnp.exp(a_cs[..., -1]).astype(dtype).transpose(3, 0, 1, 2)

    def step(h_prev, inp):
        st, dec = inp
        return h_prev * dec[..., None, None] + st, h_prev

    h0 = jnp.zeros((bsz, g, r, p, n), dtype)
    _, prev = lax.scan(step, h0, (states.transpose(1, 0, 2, 3, 4, 5), chunk_decay))
    prev = prev.transpose(1, 0, 2, 3, 4, 5)
    decay_from_start = jnp.exp(a_cs).astype(dtype).transpose(0, 3, 4, 1, 2)
    y_off = jnp.einsum('bclgn,bcgrpn->bclgrp', cc, prev) * decay_from_start[..., None]
    return (y_diag + y_off).reshape(bsz, s, h, p) + xs * d_skip[:, None].astype(dtype)


def _layer(x, c_act, mod_w, mod_b, norm_mix_g, norm_ffn_g, w_in, fox_forget_b, attn_norm_g,
           ssd_conv_w, ssd_conv_b, ssd_dt_bias, ssd_a_log, ssd_d, ssd_norm_g, w_out,
           ffn_w_up, ffn_conv_w, ffn_conv_b, ffn_w_down):
    bsz, s, _ = x.shape
    mod = (c_act @ mod_w + mod_b)[:, None, :]
    shift_m, scale_m, gate_m, shift_f, scale_f, gate_f = jnp.split(mod, N_MOD, axis=-1)

    h = _rmsnorm(x, norm_mix_g) * (1.0 + scale_m) + shift_m
    proj = h @ w_in
    sizes = [D_ATT, D_ATT, D_ATT, ATT_HEADS, D_SSD, D_CONV_CH, SSD_HEADS]
    q, k, v, f_logit, z, xbc, dt_raw = jnp.split(proj, list(np.cumsum(sizes)[:-1]), axis=-1)

    log_f = jax.nn.log_sigmoid(f_logit.astype(jnp.float32) + fox_forget_b.astype(jnp.float32))
    shp = (bsz, s, ATT_HEADS, ATT_HEAD_DIM)
    y_att = _forgetting_attention(q.reshape(shp), k.reshape(shp), v.reshape(shp), log_f)
    y_att = _rmsnorm(y_att, attn_norm_g)

    xbc = jax.nn.silu(_causal_dwconv(xbc, ssd_conv_w, ssd_conv_b))
    xs, b_in, c_in = jnp.split(xbc, [D_SSD, D_SSD + SSD_GROUPS * SSD_STATE], axis=-1)
    dt = jax.nn.softplus(dt_raw.astype(jnp.float32) + ssd_dt_bias.astype(jnp.float32))
    a_neg = -jnp.exp(ssd_a_log.astype(jnp.float32))
    y_ssd = _ssd_chunked(xs.reshape(bsz, s, SSD_HEADS, SSD_HEAD_DIM), dt, a_neg,
                         b_in.reshape(bsz, s, SSD_GROUPS, SSD_STATE),
                         c_in.reshape(bsz, s, SSD_GROUPS, SSD_STATE), ssd_d)
    y_ssd = y_ssd.reshape(bsz, s, D_SSD) * jax.nn.silu(z)
    y_ssd = _rmsnorm(y_ssd.reshape(bsz, s, SSD_GROUPS, D_SSD // SSD_GROUPS),
                     ssd_norm_g.reshape(SSD_GROUPS, D_SSD // SSD_GROUPS)).reshape(bsz, s, D_SSD)

    y = jnp.concatenate([y_att, y_ssd], axis=-1) @ w_out
    x = x + gate_m * y

    h = _rmsnorm(x, norm_ffn_g) * (1.0 + scale_f) + shift_f
    u = _causal_dwconv(h @ ffn_w_up, ffn_conv_w, ffn_conv_b)
    u_gate, u_val = jnp.split(u, 2, axis=-1)
    y = (jax.nn.silu(u_gate) * u_val) @ ffn_w_down
    return x + gate_f * y


def setup_inputs(seed: int = 0) -> dict:
    key = jax.random.key(seed)
    ks = jax.random.split(key, 24)
    nrm = jax.random.normal
    f32 = jnp.float32
    dt0 = jnp.exp(jax.random.uniform(ks[11], (DEPTH, SSD_HEADS), f32, np.log(1e-3), np.log(1e-1)))
    return {
        "x": nrm(ks[0], (BATCH, SEQ, D_MODEL), f32),
        "c": nrm(ks[1], (BATCH, D_MODEL), f32),
        "mod_w": nrm(ks[2], (DEPTH, D_MODEL, N_MOD * D_MODEL), f32) * (0.5 * D_MODEL ** -0.5),
        "mod_b": 0.01 * nrm(ks[3], (DEPTH, N_MOD * D_MODEL), f32),
        "norm_mix_g": 1.0 + 0.02 * nrm(ks[4], (DEPTH, D_MODEL), f32),
        "norm_ffn_g": 1.0 + 0.02 * nrm(ks[5], (DEPTH, D_MODEL), f32),
        "w_in": nrm(ks[6], (DEPTH, D_MODEL, D_IN), f32) * D_MODEL ** -0.5,
        "fox_forget_b": 3.0 + 0.5 * nrm(ks[7], (DEPTH, ATT_HEADS), f32),
        "attn_norm_g": 1.0 + 0.02 * nrm(ks[8], (DEPTH, D_ATT), f32),
        "ssd_conv_w": nrm(ks[9], (DEPTH, SSD_CONV, D_CONV_CH), f32) * SSD_CONV ** -0.5,
        "ssd_conv_b": 0.01 * nrm(ks[10], (DEPTH, D_CONV_CH), f32),
        "ssd_dt_bias": dt0 + jnp.log(-jnp.expm1(-dt0)),
        "ssd_a_log": jnp.log(jax.random.uniform(ks[12], (DEPTH, SSD_HEADS), f32, 1.0, 16.0)),
        "ssd_d": 1.0 + 0.1 * nrm(ks[13], (DEPTH, SSD_HEADS), f32),
        "ssd_norm_g": 1.0 + 0.02 * nrm(ks[14], (DEPTH, D_SSD), f32),
        "w_out": nrm(ks[15], (DEPTH, D_MIX, D_MODEL), f32) * D_MIX ** -0.5,
        "ffn_w_up": nrm(ks[16], (DEPTH, D_MODEL, 2 * D_FF), f32) * D_MODEL ** -0.5,
        "ffn_conv_w": nrm(ks[17], (DEPTH, FFN_CONV, 2 * D_FF), f32) * FFN_CONV ** -0.5,
        "ffn_conv_b": 0.01 * nrm(ks[18], (DEPTH, 2 * D_FF), f32),
        "ffn_w_down": nrm(ks[19], (DEPTH, D_FF, D_MODEL), f32) * D_FF ** -0.5,
        "final_g": 1.0 + 0.02 * nrm(ks[20], (D_MODEL,), f32),
    }


def reference(x, c, mod_w, mod_b, norm_mix_g, norm_ffn_g, w_in, fox_forget_b, attn_norm_g,
              ssd_conv_w, ssd_conv_b, ssd_dt_bias, ssd_a_log, ssd_d, ssd_norm_g, w_out,
              ffn_w_up, ffn_conv_w, ffn_conv_b, ffn_w_down, final_g):
    c_act = jax.nn.silu(c)
    for l in range(DEPTH):
        x = _layer(x, c_act, mod_w[l], mod_b[l], norm_mix_g[l], norm_ffn_g[l], w_in[l],
                   fox_forget_b[l], attn_norm_g[l], ssd_conv_w[l], ssd_conv_b[l],
                   ssd_dt_bias[l], ssd_a_log[l], ssd_d[l], ssd_norm_g[l], w_out[l],
                   ffn_w_up[l], ffn_conv_w[l], ffn_conv_b[l], ffn_w_down[l])
    return _rmsnorm(x, final_g)
```

```python
import functools

import numpy as np
import jax
import jax.numpy as jnp
from jax import lax
from jax.experimental import pallas as pl
from jax.experimental.pallas import tpu as pltpu

F32 = jnp.float32
BF16 = jnp.bfloat16

D_MODEL = 1024
N_HEADS = 8
HEAD_DIM = 64
D_ATT = 512
D_SSD = 512
SSD_GROUPS = 2
SSD_STATE = 128
SSD_CONV = 4
SSD_CHUNK = 128
D_XBC = D_SSD + 2 * SSD_GROUPS * SSD_STATE
D_FF = 2816
FFN_CONV = 3
N_MOD = 6
EPS = 1e-6
NEG = -1e30

LANES = 128
BF16_ROWS = 16
VMEM_LIMIT = 56 * 1024 * 1024

TM = 512
TQ = 512
TK = 512
TS = 512
TF = 256

F_HI, DT_LO, F_MID, F_LO = 0, 8, 16, 24


def _sigmoid(x):
    return 1.0 / (1.0 + jnp.exp(-x))


def _softplus(x):
    return jnp.maximum(x, 0.0) + jnp.log1p(jnp.exp(-jnp.abs(x)))


def _split2(x):
    hi = x.astype(BF16)
    lo = (x - hi.astype(F32)).astype(BF16)
    return hi, lo


def _split3(x):
    hi = x.astype(BF16)
    r = x - hi.astype(F32)
    mid = r.astype(BF16)
    lo = (r - mid.astype(F32)).astype(BF16)
    return hi, mid, lo


def _dot(a, b):
    return jnp.dot(a, b, preferred_element_type=F32)


def _dot_nt(a, b):
    return lax.dot_general(a, b, (((1,), (1,)), ((), ())), preferred_element_type=F32)


def _tril(n):
    r = lax.broadcasted_iota(jnp.int32, (n, n), 0)
    c = lax.broadcasted_iota(jnp.int32, (n, n), 1)
    return jnp.where(c <= r, 1.0, 0.0).astype(BF16)


def _rms(x):
    return x * lax.rsqrt(jnp.mean(x * x, axis=-1, keepdims=True) + EPS)


def _mod_kernel(c_ref, w_ref, b_ref, o_ref):
    c = c_ref[...]
    o_ref[0] = _dot(c * _sigmoid(c), w_ref[0]) + b_ref[0]


def _modulation(c, mod_w, mod_b):
    depth, d, n = mod_w.shape
    bsz = c.shape[0]
    tn = 1536
    return pl.pallas_call(
        _mod_kernel,
        grid=(depth, n // tn),
        in_specs=[pl.BlockSpec((bsz, d), lambda l, j: (0, 0)),
                  pl.BlockSpec((1, d, tn), lambda l, j: (l, 0, j)),
                  pl.BlockSpec((1, 1, tn), lambda l, j: (l, 0, j))],
        out_specs=pl.BlockSpec((1, bsz, tn), lambda l, j: (l, 0, j)),
        out_shape=jax.ShapeDtypeStruct((depth, bsz, n), F32),
        compiler_params=pltpu.CompilerParams(
            dimension_semantics=("parallel", "parallel"), vmem_limit_bytes=VMEM_LIMIT),
        name="modulation",
    )(c, mod_w, mod_b.reshape(depth, 1, n))


def _inproj_kernel(x_ref, mod_ref, g_ref, wr_ref, wc_ref, bs_ref,
                   qT_ref, vT_ref, k_ref, z_ref, xbc_ref, fp_ref, dt_ref, carry_ref):
    @pl.when(pl.program_id(1) == 0)
    def _():
        carry_ref[...] = jnp.zeros_like(carry_ref)

    x = x_ref[0]
    y = _rms(x) * g_ref[...]
    h = (y * (1.0 + mod_ref[0, 1:2, :]) + mod_ref[0, 0:1, :]).astype(BF16)

    rows = _dot_nt(wr_ref[...], h)
    qT_ref[0] = rows[:D_ATT].astype(BF16)
    vT_ref[0, 0] = rows[D_ATT:].astype(BF16)
    k_ref[0] = _dot(h, wc_ref[:, 0:D_ATT]).astype(BF16)
    z_ref[0] = _dot(h, wc_ref[:, D_ATT:D_ATT + D_SSD]).astype(BF16)
    o = D_ATT + D_SSD
    xbc_ref[0] = _dot(h, wc_ref[:, o:o + D_XBC]).astype(BF16)
    small = _dot(h, wc_ref[:, o + D_XBC:o + D_XBC + LANES]) + bs_ref[...]

    t = jnp.log1p(jnp.exp(-jnp.abs(small)))
    dt_ref[0] = jnp.maximum(small, 0.0) + t
    nlf = jnp.maximum(-small, 0.0) + t

    tri = _tril(TM)
    hi, mid, lo = _split3(nlf)
    cum = _dot(tri, hi) + _dot(tri, mid) + _dot(tri, lo) + carry_ref[...]
    carry_ref[...] = cum[TM - 1:TM, :]

    hi, mid, lo = _split3(cum)
    lane = lax.broadcasted_iota(jnp.int32, cum.shape, 1) // 8
    parts = jnp.where(lane == F_HI // 8, hi.astype(F32),
                      jnp.where(lane == F_MID // 8, mid.astype(F32),
                                jnp.where(lane == F_LO // 8, lo.astype(F32), 0.0)))
    fp_ref[0] = parts.astype(BF16)


def _in_proj(x, mod, g, w_rows, w_cols, bias_small):
    bsz, s, d = x.shape
    ns = s // TM
    ncol = w_cols.shape[1]
    out_shape = (
        jax.ShapeDtypeStruct((bsz, D_ATT, s), BF16),
        jax.ShapeDtypeStruct((bsz, ns, D_ATT, TM), BF16),
        jax.ShapeDtypeStruct((bsz, s, D_ATT), BF16),
        jax.ShapeDtypeStruct((bsz, s, D_SSD), BF16),
        jax.ShapeDtypeStruct((bsz, s, D_XBC), BF16),
        jax.ShapeDtypeStruct((bsz, s, LANES), BF16),
        jax.ShapeDtypeStruct((bsz, s, LANES), F32),
    )
    return pl.pallas_call(
        _inproj_kernel,
        grid=(bsz, ns),
        in_specs=[pl.BlockSpec((1, TM, d), lambda b, i: (b, i, 0)),
                  pl.BlockSpec((1, N_MOD, d), lambda b, i: (b, 0, 0)),
                  pl.BlockSpec((1, d), lambda b, i: (0, 0)),
                  pl.BlockSpec((2 * D_ATT, d), lambda b, i: (0, 0)),
                  pl.BlockSpec((d, ncol), lambda b, i: (0, 0)),
                  pl.BlockSpec((1, LANES), lambda b, i: (0, 0))],
        out_specs=(pl.BlockSpec((1, D_ATT, TM), lambda b, i: (b, 0, i)),
                   pl.BlockSpec((1, 1, D_ATT, TM), lambda b, i: (b, i, 0, 0)),
                   pl.BlockSpec((1, TM, D_ATT), lambda b, i: (b, i, 0)),
                   pl.BlockSpec((1, TM, D_SSD), lambda b, i: (b, i, 0)),
                   pl.BlockSpec((1, TM, D_XBC), lambda b, i: (b, i, 0)),
                   pl.BlockSpec((1, TM, LANES), lambda b, i: (b, i, 0)),
                   pl.BlockSpec((1, TM, LANES), lambda b, i: (b, i, 0))),
        out_shape=out_shape,
        scratch_shapes=[pltpu.VMEM((1, LANES), F32)],
        compiler_params=pltpu.CompilerParams(
            dimension_semantics=("parallel", "arbitrary"), vmem_limit_bytes=VMEM_LIMIT),
        name="in_proj",
    )(x, mod, g, w_rows, w_cols, bias_small)


def _attn_kernel(qT_ref, k_ref, f_ref, vT_ref, o_ref):
    h = pl.program_id(1)
    i = pl.program_id(2)
    odd = h % 2
    q = qT_ref[0] * jnp.asarray(HEAD_DIM ** -0.5, BF16)
    zq = jnp.zeros_like(q)
    top = jnp.concatenate([jnp.where(odd == 0, q, zq), jnp.where(odd == 1, q, zq)], axis=0)
    r = lax.broadcasted_iota(jnp.int32, (LANES, TQ), 0)
    sel = jnp.where(r == h + F_HI, 1.0, jnp.where(r == h + F_MID, 1.0, jnp.where(r == h + F_LO, 1.0, 0.0)))
    qa = jnp.concatenate([top, sel.astype(BF16)], axis=0)

    def tile(j, carry, masked):
        m, l, acc = carry
        off = pl.multiple_of(j * TK, TK)
        ka = jnp.concatenate([k_ref[0, pl.ds(off, TK), :], f_ref[0, pl.ds(off, TK), :]], axis=1)
        st = _dot(ka, qa)
        if masked:
            rr = lax.broadcasted_iota(jnp.int32, (TK, TQ), 0)
            cc = lax.broadcasted_iota(jnp.int32, (TK, TQ), 1)
            st = jnp.where(rr <= cc, st, NEG)
        m_new = jnp.maximum(m, jnp.max(st, axis=0, keepdims=True))
        alpha = jnp.exp(m - m_new)
        p = jnp.exp(st - m_new)
        l = alpha * l + jnp.sum(p, axis=0, keepdims=True)
        acc = alpha * acc + _dot(vT_ref[0, j], p.astype(BF16))
        return m_new, l, acc

    init = (jnp.full((1, TQ), NEG, F32), jnp.zeros((1, TQ), F32), jnp.zeros((HEAD_DIM, TQ), F32))
    carry = lax.fori_loop(0, i, lambda j, c: tile(j, c, False), init)
    m, l, acc = tile(i, carry, True)
    o_ref[0] = acc * (1.0 / l)


def _attention(qT, k, fparts, vT):
    bsz, _, s = qT.shape
    nk = s // TK
    return pl.pallas_call(
        _attn_kernel,
        grid=(bsz, N_HEADS, s // TQ),
        in_specs=[pl.BlockSpec((1, HEAD_DIM, TQ), lambda b, h, i: (b, h, i)),
                  pl.BlockSpec((1, s, LANES), lambda b, h, i: (b, 0, h // 2)),
                  pl.BlockSpec((1, s, LANES), lambda b, h, i: (b, 0, 0)),
                  pl.BlockSpec((1, nk, HEAD_DIM, TK), lambda b, h, i: (b, 0, h, 0))],
        out_specs=pl.BlockSpec((1, HEAD_DIM, TQ), lambda b, h, i: (b, h, i)),
        out_shape=jax.ShapeDtypeStruct((bsz, D_ATT, s), F32),
        compiler_params=pltpu.CompilerParams(
            dimension_semantics=("parallel", "parallel", "parallel"), vmem_limit_bytes=VMEM_LIMIT),
        name="attention",
    )(qT, k, fparts, vT)


def _ssd_kernel(xbc_ref, halo_ref, z_ref, dt_ref, cw_ref, cb_ref, alog_ref, dskip_ref, gn_ref, e_ref,
                y_ref, buf_ref, state_ref):
    s = pl.program_id(1)

    @pl.when(s == 0)
    def _():
        state_ref[...] = jnp.zeros_like(state_ref)

    keep = jnp.where(s > 0, 1.0, 0.0)
    buf_ref[0:BF16_ROWS, :] = halo_ref[0].astype(F32) * keep
    buf_ref[BF16_ROWS:BF16_ROWS + TS, :] = xbc_ref[0].astype(F32)
    conv = cb_ref[...]
    for kk in range(SSD_CONV):
        conv = conv + buf_ref[pl.ds(BF16_ROWS - SSD_CONV + 1 + kk, TS), :] * cw_ref[kk:kk + 1, :]
    xc = conv * _sigmoid(conv)
    xs = xc[:, :D_SSD]
    gn = SSD_GROUPS * SSD_STATE
    bm = xc[:, D_SSD:D_SSD + gn]
    cm = xc[:, D_SSD + gn:]

    lane = lax.broadcasted_iota(jnp.int32, (1, LANES), 1) // 8
    head_lanes = lane == DT_LO // 8
    a_neg = jnp.where(head_lanes, -jnp.exp(alog_ref[...]), 0.0)
    dt_small = jnp.where(head_lanes, dt_ref[0], 0.0)
    a_small = dt_small * a_neg
    e = e_ref[...]
    dhi, dlo = _split2(dt_small)
    dt_exp = _dot(dhi, e) + _dot(dlo, e)

    tri = _tril(SSD_CHUNK)
    rr = lax.broadcasted_iota(jnp.int32, (SSD_CHUNK, SSD_CHUNK), 0)
    cc = lax.broadcasted_iota(jnp.int32, (SSD_CHUNK, SSD_CHUNK), 1)
    causal = cc <= rr
    lane2 = lax.broadcasted_iota(jnp.int32, (SSD_CHUNK, LANES), 1)
    heads_per_group = N_HEADS // SSD_GROUPS
    gw = heads_per_group * HEAD_DIM

    for c in range(TS // SSD_CHUNK):
        rows = slice(c * SSD_CHUNK, (c + 1) * SSD_CHUNK)
        ahi, alo = _split2(a_small[rows])
        cs = _dot(tri, ahi) + _dot(tri, alo)
        chi, clo = _split2(cs)
        cs_exp = _dot(chi, e) + _dot(clo, e)
        cs_t = cs.T
        last = cs_exp[SSD_CHUNK - 1:SSD_CHUNK, :]
        dec_end = jnp.exp(last - cs_exp)
        dec_start = jnp.exp(cs_exp)
        dec_chunk = jnp.exp(last)
        xs_c = xs[rows]
        xdt = xs_c * dt_exp[rows]
        xde = (xdt * dec_end).astype(BF16)
        outs = []
        for g in range(SSD_GROUPS):
            bg = bm[rows, g * SSD_STATE:(g + 1) * SSD_STATE]
            cg = cm[rows, g * SSD_STATE:(g + 1) * SSD_STATE].astype(BF16)
            cb = _dot_nt(cg, bg.astype(BF16))
            st = state_ref[:, g * gw:(g + 1) * gw]
            y_off = _dot(cg, st.astype(BF16)) * dec_start[:, g * gw:(g + 1) * gw]
            pair_out = []
            for pr in range(heads_per_group // 2):
                lo_l = g * gw + pr * LANES
                xp = xdt[:, lo_l:lo_l + LANES]
                acc = None
                for half in range(2):
                    hd = g * heads_per_group + pr * 2 + half
                    col = cs[:, DT_LO + hd:DT_LO + hd + 1]
                    row = cs_t[DT_LO + hd:DT_LO + hd + 1, :]
                    seg = jnp.where(causal, col - row, NEG)
                    mmat = (cb * jnp.exp(seg)).astype(BF16)
                    keep_half = (lane2 < HEAD_DIM) if half == 0 else (lane2 >= HEAD_DIM)
                    xh = jnp.where(keep_half, xp, 0.0).astype(BF16)
                    term = _dot(mmat, xh)
                    acc = term if acc is None else acc + term
                pair_out.append(acc)
            y_diag = jnp.concatenate(pair_out, axis=1)
            outs.append(y_diag + y_off)
            new_state = st * dec_chunk[:, g * gw:(g + 1) * gw] + _dot(bg.T.astype(BF16), xde[:, g * gw:(g + 1) * gw])
            state_ref[:, g * gw:(g + 1) * gw] = new_state
        y = jnp.concatenate(outs, axis=1) + xs_c * dskip_ref[...]
        zc = z_ref[0, rows, :].astype(F32)
        y = y * (zc * _sigmoid(zc))
        normed = [_rms(y[:, g * gw:(g + 1) * gw]) for g in range(SSD_GROUPS)]
        y_ref[0, rows, :] = (jnp.concatenate(normed, axis=1) * gn_ref[...]).astype(BF16)


def _ssd(xbc, z, dt, conv_w, conv_b, alog_row, dskip_row, gnorm_row, expand):
    bsz, s, _ = xbc.shape
    hb = TS // BF16_ROWS
    const = lambda b, i: (0, 0)
    return pl.pallas_call(
        _ssd_kernel,
        grid=(bsz, s // TS),
        in_specs=[pl.BlockSpec((1, TS, D_XBC), lambda b, i: (b, i, 0)),
                  pl.BlockSpec((1, BF16_ROWS, D_XBC), lambda b, i: (b, jnp.maximum(i * hb - 1, 0), 0)),
                  pl.BlockSpec((1, TS, D_SSD), lambda b, i: (b, i, 0)),
                  pl.BlockSpec((1, TS, LANES), lambda b, i: (b, i, 0)),
                  pl.BlockSpec((SSD_CONV, D_XBC), const),
                  pl.BlockSpec((1, D_XBC), const),
                  pl.BlockSpec((1, LANES), const),
                  pl.BlockSpec((1, D_SSD), const),
                  pl.BlockSpec((1, D_SSD), const),
                  pl.BlockSpec((LANES, D_SSD), const)],
        out_specs=pl.BlockSpec((1, TS, D_SSD), lambda b, i: (b, i, 0)),
        out_shape=jax.ShapeDtypeStruct((bsz, s, D_SSD), BF16),
        scratch_shapes=[pltpu.VMEM((BF16_ROWS + TS, D_XBC), F32),
                        pltpu.VMEM((SSD_STATE, D_SSD), F32)],
        compiler_params=pltpu.CompilerParams(
            dimension_semantics=("parallel", "arbitrary"), vmem_limit_bytes=VMEM_LIMIT),
        name="ssd",
    )(xbc, xbc, z, dt, conv_w, conv_b, alog_row, dskip_row, gnorm_row, expand)


def _outproj_kernel(yT_ref, ys_ref, x_ref, mod_ref, ga_ref, w_ref, gf_ref, x1_ref, h2_ref):
    ya = _rms(yT_ref[0].T) * ga_ref[...]
    ycat = jnp.concatenate([ya.astype(BF16), ys_ref[0]], axis=1)
    x1 = x_ref[0] + mod_ref[0, 2:3, :] * _dot(ycat, w_ref[...])
    x1_ref[0] = x1
    h2 = _rms(x1) * gf_ref[...]
    h2_ref[0] = (h2 * (1.0 + mod_ref[0, 4:5, :]) + mod_ref[0, 3:4, :]).astype(BF16)


def _out_proj(yT, yssd, x, mod, g_att, w_out, g_ffn):
    bsz, s, d = x.shape
    const = lambda b, i: (0, 0)
    return pl.pallas_call(
        _outproj_kernel,
        grid=(bsz, s // TM),
        in_specs=[pl.BlockSpec((1, D_ATT, TM), lambda b, i: (b, 0, i)),
                  pl.BlockSpec((1, TM, D_SSD), lambda b, i: (b, i, 0)),
                  pl.BlockSpec((1, TM, d), lambda b, i: (b, i, 0)),
                  pl.BlockSpec((1, N_MOD, d), lambda b, i: (b, 0, 0)),
                  pl.BlockSpec((1, D_ATT), const),
                  pl.BlockSpec((D_ATT + D_SSD, d), const),
                  pl.BlockSpec((1, d), const)],
        out_specs=(pl.BlockSpec((1, TM, d), lambda b, i: (b, i, 0)),
                   pl.BlockSpec((1, TM, d), lambda b, i: (b, i, 0))),
        out_shape=(jax.ShapeDtypeStruct((bsz, s, d), F32), jax.ShapeDtypeStruct((bsz, s, d), BF16)),
        compiler_params=pltpu.CompilerParams(
            dimension_semantics=("parallel", "parallel"), vmem_limit_bytes=VMEM_LIMIT),
        name="out_proj",
    )(yT, yssd, x, mod, g_att, w_out, g_ffn)


def _ffn_kernel(h_ref, halo_ref, x1_ref, mod_ref, wu_ref, cw_ref, cb_ref, wd_ref, gfin_ref,
                o_ref, bufg_ref, bufv_ref, *, final_norm):
    keep = jnp.where(pl.program_id(1) > 0, 1.0, 0.0).astype(BF16)
    hext = jnp.concatenate([halo_ref[0] * keep, h_ref[0]], axis=0)
    first = BF16_ROWS - FFN_CONV + 1
    acc = jnp.zeros((TM, D_MODEL), F32)
    for j in range(D_FF // TF):
        cols_g = slice(j * TF, (j + 1) * TF)
        cols_v = slice(D_FF + j * TF, D_FF + (j + 1) * TF)
        bufg_ref[...] = _dot(hext, wu_ref[:, cols_g])
        bufv_ref[...] = _dot(hext, wu_ref[:, cols_v])
        ug = cb_ref[:, cols_g]
        uv = cb_ref[:, cols_v]
        for kk in range(FFN_CONV):
            ug = ug + bufg_ref[pl.ds(first + kk, TM), :] * cw_ref[kk:kk + 1, cols_g]
            uv = uv + bufv_ref[pl.ds(first + kk, TM), :] * cw_ref[kk:kk + 1, cols_v]
        act = (ug * _sigmoid(ug) * uv).astype(BF16)
        acc = acc + _dot(act, wd_ref[cols_g, :])
    x2 = x1_ref[0] + mod_ref[0, 5:6, :] * acc
    if final_norm:
        x2 = _rms(x2) * gfin_ref[...]
    o_ref[0] = x2


def _ffn(h2, x1, mod, w_up, conv_w, conv_b, w_down, g_final, final_norm):
    bsz, s, d = x1.shape
    hb = TM // BF16_ROWS
    const = lambda b, i: (0, 0)
    return pl.pallas_call(
        functools.partial(_ffn_kernel, final_norm=final_norm),
        grid=(bsz, s // TM),
        in_specs=[pl.BlockSpec((1, TM, d), lambda b, i: (b, i, 0)),
                  pl.BlockSpec((1, BF16_ROWS, d), lambda b, i: (b, jnp.maximum(i * hb - 1, 0), 0)),
                  pl.BlockSpec((1, TM, d), lambda b, i: (b, i, 0)),
                  pl.BlockSpec((1, N_MOD, d), lambda b, i: (b, 0, 0)),
                  pl.BlockSpec((d, 2 * D_FF), const, pipeline_mode=pl.Buffered(1)),
                  pl.BlockSpec((FFN_CONV, 2 * D_FF), const),
                  pl.BlockSpec((1, 2 * D_FF), const),
                  pl.BlockSpec((D_FF, d), const, pipeline_mode=pl.Buffered(1)),
                  pl.BlockSpec((1, d), const)],
        out_specs=pl.BlockSpec((1, TM, d), lambda b, i: (b, i, 0)),
        out_shape=jax.ShapeDtypeStruct((bsz, s, d), F32),
        scratch_shapes=[pltpu.VMEM((BF16_ROWS + TM, TF), F32),
                        pltpu.VMEM((BF16_ROWS + TM, TF), F32)],
        compiler_params=pltpu.CompilerParams(
            dimension_semantics=("parallel", "parallel"), vmem_limit_bytes=VMEM_LIMIT),
        name="ffn",
    )(h2, h2, x1, mod, w_up, conv_w, conv_b, w_down, g_final)


def _expand_matrix():
    e = np.zeros((LANES, D_SSD), np.float32)
    for hd in range(N_HEADS):
        e[DT_LO + hd, hd * HEAD_DIM:(hd + 1) * HEAD_DIM] = 1.0
    return jnp.asarray(e, BF16)


def _small_row(f_vals, dt_vals):
    row = jnp.zeros((LANES,), F32)
    row = row.at[F_HI:F_HI + N_HEADS].set(f_vals)
    row = row.at[F_MID:F_MID + N_HEADS].set(f_vals)
    row = row.at[F_LO:F_LO + N_HEADS].set(f_vals)
    row = row.at[DT_LO:DT_LO + N_HEADS].set(dt_vals)
    return row.reshape(1, LANES)


def kernel(x, c, mod_w, mod_b, norm_mix_g, norm_ffn_g, w_in, fox_forget_b, attn_norm_g, ssd_conv_w, ssd_conv_b, ssd_dt_bias, ssd_a_log, ssd_d, ssd_norm_g, w_out, ffn_w_up, ffn_conv_w, ffn_conv_b, ffn_w_down, final_g):
    depth = w_in.shape[0]
    bsz = x.shape[0]
    mod_all = _modulation(c, mod_w, mod_b).reshape(depth, bsz, N_MOD, D_MODEL)
    expand = _expand_matrix()
    zeros8 = jnp.zeros((N_HEADS,), F32)

    o_k, o_v, o_f = D_ATT, 2 * D_ATT, 3 * D_ATT
    o_z = o_f + N_HEADS
    o_x = o_z + D_SSD
    o_dt = o_x + D_XBC

    for l in range(depth):
        w = w_in[l]
        w_f = w[:, o_f:o_z]
        w_dt = w[:, o_dt:o_dt + N_HEADS]
        w_small = jnp.zeros((D_MODEL, LANES), F32)
        w_small = w_small.at[:, F_HI:F_HI + N_HEADS].set(w_f).at[:, F_MID:F_MID + N_HEADS].set(w_f)
        w_small = w_small.at[:, F_LO:F_LO + N_HEADS].set(w_f).at[:, DT_LO:DT_LO + N_HEADS].set(w_dt)
        w_rows = jnp.concatenate([w[:, :o_k], w[:, o_v:o_f]], axis=1).T.astype(BF16)
        w_cols = jnp.concatenate([w[:, o_k:o_v], w[:, o_z:o_x], w[:, o_x:o_dt], w_small], axis=1).astype(BF16)
        bias_small = _small_row(fox_forget_b[l], ssd_dt_bias[l])
        mod = mod_all[l]

        qT, vT, k, z, xbc, fparts, dt = _in_proj(
            x, mod, norm_mix_g[l].reshape(1, -1), w_rows, w_cols, bias_small)
        yT = _attention(qT, k, fparts, vT)
        yssd = _ssd(xbc, z, dt, ssd_conv_w[l], ssd_conv_b[l].reshape(1, -1),
                    _small_row(zeros8, ssd_a_log[l]),
                    jnp.repeat(ssd_d[l], HEAD_DIM).reshape(1, -1),
                    ssd_norm_g[l].reshape(1, -1), expand)
        x1, h2 = _out_proj(yT, yssd, x, mod, attn_norm_g[l].reshape(1, -1),
                           w_out[l].astype(BF16), norm_ffn_g[l].reshape(1, -1))
        x = _ffn(h2, x1, mod, ffn_w_up[l].astype(BF16), ffn_conv_w[l], ffn_conv_b[l].reshape(1, -1),
                 ffn_w_down[l].astype(BF16), final_g.reshape(1, -1), final_norm=(l == depth - 1))
    return x
```

```python
import functools

import numpy as np
import jax
import jax.numpy as jnp
from jax import lax
from jax.experimental import pallas as pl
from jax.experimental.pallas import tpu as pltpu

F32 = jnp.float32
BF16 = jnp.bfloat16

D_MODEL = 1024
N_HEADS = 8
HEAD_DIM = 64
D_ATT = 512
D_SSD = 512
SSD_GROUPS = 2
SSD_STATE = 128
SSD_CONV = 4
SSD_CHUNK = 128
D_XBC = D_SSD + 2 * SSD_GROUPS * SSD_STATE
D_FF = 2816
FFN_CONV = 3
N_MOD = 6
EPS = 1e-6
NEG = -1e30

LANES = 128
BF16_ROWS = 16
VMEM_LIMIT = 56 * 1024 * 1024

TM = 512
TQ = 512
TK = 512
TS = 512
TF = 256

F_HI, DT_LO, F_MID, F_LO = 0, 8, 16, 24


def _sigmoid(x):
    return 1.0 / (1.0 + jnp.exp(-x))


def _softplus(x):
    return jnp.maximum(x, 0.0) + jnp.log1p(jnp.exp(-jnp.abs(x)))


def _split2(x):
    hi = x.astype(BF16)
    lo = (x - hi.astype(F32)).astype(BF16)
    return hi, lo


def _split3(x):
    hi = x.astype(BF16)
    r = x - hi.astype(F32)
    mid = r.astype(BF16)
    lo = (r - mid.astype(F32)).astype(BF16)
    return hi, mid, lo


def _dot(a, b):
    return jnp.dot(a, b, preferred_element_type=F32)


def _dot_nt(a, b):
    return lax.dot_general(a, b, (((1,), (1,)), ((), ())), preferred_element_type=F32)


def _tril(n):
    r = lax.broadcasted_iota(jnp.int32, (n, n), 0)
    c = lax.broadcasted_iota(jnp.int32, (n, n), 1)
    return jnp.where(c <= r, 1.0, 0.0).astype(BF16)


def _rms(x):
    return x * lax.rsqrt(jnp.mean(x * x, axis=-1, keepdims=True) + EPS)


def _mod_kernel(c_ref, w_ref, b_ref, o_ref):
    c = c_ref[...]
    o_ref[0] = _dot(c * _sigmoid(c), w_ref[0]) + b_ref[0]


def _modulation(c, mod_w, mod_b):
    depth, d, n = mod_w.shape
    bsz = c.shape[0]
    tn = 1536
    return pl.pallas_call(
        _mod_kernel,
        grid=(depth, n // tn),
        in_specs=[pl.BlockSpec((bsz, d), lambda l, j: (0, 0)),
                  pl.BlockSpec((1, d, tn), lambda l, j: (l, 0, j)),
                  pl.BlockSpec((1, 1, tn), lambda l, j: (l, 0, j))],
        out_specs=pl.BlockSpec((1, bsz, tn), lambda l, j: (l, 0, j)),
        out_shape=jax.ShapeDtypeStruct((depth, bsz, n), F32),
        compiler_params=pltpu.CompilerParams(
            dimension_semantics=("parallel", "parallel"), vmem_limit_bytes=VMEM_LIMIT),
        name="modulation",
    )(c, mod_w, mod_b.reshape(depth, 1, n))


def _inproj_kernel(x_ref, mod_ref, g_ref, wr_ref, wc_ref, bs_ref,
                   qT_ref, vT_ref, k_ref, z_ref, xbc_ref, fp_ref, dt_ref, carry_ref):
    @pl.when(pl.program_id(1) == 0)
    def _():
        carry_ref[...] = jnp.zeros_like(carry_ref)

    x = x_ref[0]
    y = _rms(x) * g_ref[...]
    h = (y * (1.0 + mod_ref[0, 1:2, :]) + mod_ref[0, 0:1, :]).astype(BF16)

    rows = _dot_nt(wr_ref[...], h)
    qT_ref[0] = rows[:D_ATT].astype(BF16)
    vT_ref[0, 0] = rows[D_ATT:].astype(BF16)
    k_ref[0] = _dot(h, wc_ref[:, 0:D_ATT]).astype(BF16)
    z_ref[0] = _dot(h, wc_ref[:, D_ATT:D_ATT + D_SSD]).astype(BF16)
    o = D_ATT + D_SSD
    xbc_ref[0] = _dot(h, wc_ref[:, o:o + D_XBC]).astype(BF16)
    small = _dot(h, wc_ref[:, o + D_XBC:o + D_XBC + LANES]) + bs_ref[...]

    t = jnp.log1p(jnp.exp(-jnp.abs(small)))
    dt_ref[0] = jnp.maximum(small, 0.0) + t
    nlf = jnp.maximum(-small, 0.0) + t

    tri = _tril(TM)
    hi, mid, lo = _split3(nlf)
    cum = _dot(tri, hi) + _dot(tri, mid) + _dot(tri, lo) + carry_ref[...]
    carry_ref[...] = cum[TM - 1:TM, :]

    hi, mid, lo = _split3(cum)
    lane = lax.broadcasted_iota(jnp.int32, cum.shape, 1) // 8
    parts = jnp.where(lane == F_HI // 8, hi.astype(F32),
                      jnp.where(lane == F_MID // 8, mid.astype(F32),
                                jnp.where(lane == F_LO // 8, lo.astype(F32), 0.0)))
    fp_ref[0] = parts.astype(BF16)


def _in_proj(x, mod, g, w_rows, w_cols, bias_small):
    bsz, s, d = x.shape
    ns = s // TM
    ncol = w_cols.shape[1]
    out_shape = (
        jax.ShapeDtypeStruct((bsz, D_ATT, s), BF16),
        jax.ShapeDtypeStruct((bsz, ns, D_ATT, TM), BF16),
        jax.ShapeDtypeStruct((bsz, s, D_ATT), BF16),
        jax.ShapeDtypeStruct((bsz, s, D_SSD), BF16),
        jax.ShapeDtypeStruct((bsz, s, D_XBC), BF16),
        jax.ShapeDtypeStruct((bsz, s, LANES), BF16),
        jax.ShapeDtypeStruct((bsz, s, LANES), F32),
    )
    return pl.pallas_call(
        _inproj_kernel,
        grid=(bsz, ns),
        in_specs=[pl.BlockSpec((1, TM, d), lambda b, i: (b, i, 0)),
                  pl.BlockSpec((1, N_MOD, d), lambda b, i: (b, 0, 0)),
                  pl.BlockSpec((1, d), lambda b, i: (0, 0)),
                  pl.BlockSpec((2 * D_ATT, d), lambda b, i: (0, 0)),
                  pl.BlockSpec((d, ncol), lambda b, i: (0, 0)),
                  pl.BlockSpec((1, LANES), lambda b, i: (0, 0))],
        out_specs=(pl.BlockSpec((1, D_ATT, TM), lambda b, i: (b, 0, i)),
                   pl.BlockSpec((1, 1, D_ATT, TM), lambda b, i: (b, i, 0, 0)),
                   pl.BlockSpec((1, TM, D_ATT), lambda b, i: (b, i, 0)),
                   pl.BlockSpec((1, TM, D_SSD), lambda b, i: (b, i, 0)),
                   pl.BlockSpec((1, TM, D_XBC), lambda b, i: (b, i, 0)),
                   pl.BlockSpec((1, TM, LANES), lambda b, i: (b, i, 0)),
                   pl.BlockSpec((1, TM, LANES), lambda b, i: (b, i, 0))),
        out_shape=out_shape,
        scratch_shapes=[pltpu.VMEM((1, LANES), F32)],
        compiler_params=pltpu.CompilerParams(
            dimension_semantics=("parallel", "arbitrary"), vmem_limit_bytes=VMEM_LIMIT),
        name="in_proj",
    )(x, mod, g, w_rows, w_cols, bias_small)


def _attn_kernel(qT_ref, k_ref, f_ref, vT_ref, o_ref, s0_ref, s1_ref, m_ref, l_ref, acc_ref):
    h = pl.program_id(1)
    i = pl.program_id(2)
    odd = h % 2
    q = qT_ref[0] * jnp.asarray(HEAD_DIM ** -0.5, BF16)
    zq = jnp.zeros_like(q)
    top = jnp.concatenate([jnp.where(odd == 0, q, zq), jnp.where(odd == 1, q, zq)], axis=0)
    r = lax.broadcasted_iota(jnp.int32, (LANES, TQ), 0)
    sel = jnp.where(r == h + F_HI, 1.0, jnp.where(r == h + F_MID, 1.0, jnp.where(r == h + F_LO, 1.0, 0.0)))
    qa = jnp.concatenate([top, sel.astype(BF16)], axis=0)

    def logits(t, dst):
        off = pl.multiple_of(t * TK, TK)
        ka = jnp.concatenate([k_ref[0, pl.ds(off, TK), :], f_ref[0, pl.ds(off, TK), :]], axis=1)
        dst[...] = _dot(ka, qa)

    def update(src, t, masked):
        st = src[...]
        if masked:
            rr = lax.broadcasted_iota(jnp.int32, (TK, TQ), 0)
            cc = lax.broadcasted_iota(jnp.int32, (TK, TQ), 1)
            st = jnp.where(rr <= cc, st, NEG)
        m = m_ref[...]
        m_new = jnp.maximum(m, jnp.max(st, axis=0, keepdims=True))
        alpha = jnp.exp(m - m_new)
        p = jnp.exp(st - m_new)
        l_ref[...] = alpha * l_ref[...] + jnp.sum(p, axis=0, keepdims=True)
        acc_ref[...] = alpha * acc_ref[...] + _dot(vT_ref[0, t], p.astype(BF16))
        m_ref[...] = m_new

    m_ref[...] = jnp.full_like(m_ref, NEG)
    l_ref[...] = jnp.zeros_like(l_ref)
    acc_ref[...] = jnp.zeros_like(acc_ref)

    logits(0, s0_ref)

    def pair(p, carry):
        t = 2 * p
        logits(t + 1, s1_ref)
        update(s0_ref, t, False)
        logits(t + 2, s0_ref)
        update(s1_ref, t + 1, False)
        return carry

    lax.fori_loop(0, i // 2, pair, 0)

    @pl.when(i % 2 == 1)
    def _():
        logits(i, s1_ref)
        update(s0_ref, i - 1, False)
        update(s1_ref, i, True)

    @pl.when(i % 2 == 0)
    def _():
        update(s0_ref, i, True)

    o_ref[0] = acc_ref[...] * (1.0 / l_ref[...])


def _attention(qT, k, fparts, vT):
    bsz, _, s = qT.shape
    nk = s // TK
    return pl.pallas_call(
        _attn_kernel,
        grid=(bsz, N_HEADS, s // TQ),
        in_specs=[pl.BlockSpec((1, HEAD_DIM, TQ), lambda b, h, i: (b, h, i)),
                  pl.BlockSpec((1, s, LANES), lambda b, h, i: (b, 0, h // 2)),
                  pl.BlockSpec((1, s, LANES), lambda b, h, i: (b, 0, 0)),
                  pl.BlockSpec((1, nk, HEAD_DIM, TK), lambda b, h, i: (b, 0, h, 0))],
        out_specs=pl.BlockSpec((1, HEAD_DIM, TQ), lambda b, h, i: (b, h, i)),
        out_shape=jax.ShapeDtypeStruct((bsz, D_ATT, s), F32),
        scratch_shapes=[pltpu.VMEM((TK, TQ), F32), pltpu.VMEM((TK, TQ), F32),
                        pltpu.VMEM((1, TQ), F32), pltpu.VMEM((1, TQ), F32),
                        pltpu.VMEM((HEAD_DIM, TQ), F32)],
        compiler_params=pltpu.CompilerParams(
            dimension_semantics=("parallel", "parallel", "parallel"), vmem_limit_bytes=VMEM_LIMIT),
        name="attention",
    )(qT, k, fparts, vT)


def _ssd_kernel(xbc_ref, halo_ref, z_ref, dt_ref, cw_ref, cb_ref, alog_ref, dskip_ref, gn_ref, e_ref,
                y_ref, buf_ref, state_ref):
    s = pl.program_id(1)

    @pl.when(s == 0)
    def _():
        state_ref[...] = jnp.zeros_like(state_ref)

    keep = jnp.where(s > 0, 1.0, 0.0)
    buf_ref[0:BF16_ROWS, :] = halo_ref[0].astype(F32) * keep
    buf_ref[BF16_ROWS:BF16_ROWS + TS, :] = xbc_ref[0].astype(F32)
    conv = cb_ref[...]
    for kk in range(SSD_CONV):
        conv = conv + buf_ref[pl.ds(BF16_ROWS - SSD_CONV + 1 + kk, TS), :] * cw_ref[kk:kk + 1, :]
    xc = conv * _sigmoid(conv)
    xs = xc[:, :D_SSD]
    gn = SSD_GROUPS * SSD_STATE
    bm = xc[:, D_SSD:D_SSD + gn]
    cm = xc[:, D_SSD + gn:]

    lane = lax.broadcasted_iota(jnp.int32, (1, LANES), 1) // 8
    head_lanes = lane == DT_LO // 8
    a_neg = jnp.where(head_lanes, -jnp.exp(alog_ref[...]), 0.0)
    dt_small = jnp.where(head_lanes, dt_ref[0], 0.0)
    a_small = dt_small * a_neg
    e = e_ref[...]
    dhi, dlo = _split2(dt_small)
    dt_exp = _dot(dhi, e) + _dot(dlo, e)

    tri = _tril(SSD_CHUNK)
    rr = lax.broadcasted_iota(jnp.int32, (SSD_CHUNK, SSD_CHUNK), 0)
    cc = lax.broadcasted_iota(jnp.int32, (SSD_CHUNK, SSD_CHUNK), 1)
    causal = cc <= rr
    lane2 = lax.broadcasted_iota(jnp.int32, (SSD_CHUNK, LANES), 1)
    heads_per_group = N_HEADS // SSD_GROUPS
    gw = heads_per_group * HEAD_DIM

    for c in range(TS // SSD_CHUNK):
        rows = slice(c * SSD_CHUNK, (c + 1) * SSD_CHUNK)
        ahi, alo = _split2(a_small[rows])
        cs = _dot(tri, ahi) + _dot(tri, alo)
        chi, clo = _split2(cs)
        cs_exp = _dot(chi, e) + _dot(clo, e)
        cs_t = cs.T
        last = cs_exp[SSD_CHUNK - 1:SSD_CHUNK, :]
        dec_end = jnp.exp(last - cs_exp)
        dec_start = jnp.exp(cs_exp)
        dec_chunk = jnp.exp(last)
        xs_c = xs[rows]
        xdt = xs_c * dt_exp[rows]
        xde = (xdt * dec_end).astype(BF16)
        outs = []
        for g in range(SSD_GROUPS):
            bg = bm[rows, g * SSD_STATE:(g + 1) * SSD_STATE]
            cg = cm[rows, g * SSD_STATE:(g + 1) * SSD_STATE].astype(BF16)
            cb = _dot_nt(cg, bg.astype(BF16))
            st = state_ref[:, g * gw:(g + 1) * gw]
            y_off = _dot(cg, st.astype(BF16)) * dec_start[:, g * gw:(g + 1) * gw]
            pair_out = []
            for pr in range(heads_per_group // 2):
                lo_l = g * gw + pr * LANES
                xp = xdt[:, lo_l:lo_l + LANES]
                acc = None
                for half in range(2):
                    hd = g * heads_per_group + pr * 2 + half
                    col = cs[:, DT_LO + hd:DT_LO + hd + 1]
                    row = cs_t[DT_LO + hd:DT_LO + hd + 1, :]
                    seg = jnp.where(causal, col - row, NEG)
                    mmat = (cb * jnp.exp(seg)).astype(BF16)
                    keep_half = (lane2 < HEAD_DIM) if half == 0 else (lane2 >= HEAD_DIM)
                    xh = jnp.where(keep_half, xp, 0.0).astype(BF16)
                    term = _dot(mmat, xh)
                    acc = term if acc is None else acc + term
                pair_out.append(acc)
            y_diag = jnp.concatenate(pair_out, axis=1)
            outs.append(y_diag + y_off)
            new_state = st * dec_chunk[:, g * gw:(g + 1) * gw] + _dot(bg.T.astype(BF16), xde[:, g * gw:(g + 1) * gw])
            state_ref[:, g * gw:(g + 1) * gw] = new_state
        y = jnp.concatenate(outs, axis=1) + xs_c * dskip_ref[...]
        zc = z_ref[0, rows, :].astype(F32)
        y = y * (zc * _sigmoid(zc))
        normed = [_rms(y[:, g * gw:(g + 1) * gw]) for g in range(SSD_GROUPS)]
        y_ref[0, rows, :] = (jnp.concatenate(normed, axis=1) * gn_ref[...]).astype(BF16)


def _ssd(xbc, z, dt, conv_w, conv_b, alog_row, dskip_row, gnorm_row, expand):
    bsz, s, _ = xbc.shape
    hb = TS // BF16_ROWS
    const = lambda b, i: (0, 0)
    return pl.pallas_call(
        _ssd_kernel,
        grid=(bsz, s // TS),
        in_specs=[pl.BlockSpec((1, TS, D_XBC), lambda b, i: (b, i, 0)),
                  pl.BlockSpec((1, BF16_ROWS, D_XBC), lambda b, i: (b, jnp.maximum(i * hb - 1, 0), 0)),
                  pl.BlockSpec((1, TS, D_SSD), lambda b, i: (b, i, 0)),
                  pl.BlockSpec((1, TS, LANES), lambda b, i: (b, i, 0)),
                  pl.BlockSpec((SSD_CONV, D_XBC), const),
                  pl.BlockSpec((1, D_XBC), const),
                  pl.BlockSpec((1, LANES), const),
                  pl.BlockSpec((1, D_SSD), const),
                  pl.BlockSpec((1, D_SSD), const),
                  pl.BlockSpec((LANES, D_SSD), const)],
        out_specs=pl.BlockSpec((1, TS, D_SSD), lambda b, i: (b, i, 0)),
        out_shape=jax.ShapeDtypeStruct((bsz, s, D_SSD), BF16),
        scratch_shapes=[pltpu.VMEM((BF16_ROWS + TS, D_XBC), F32),
                        pltpu.VMEM((SSD_STATE, D_SSD), F32)],
        compiler_params=pltpu.CompilerParams(
            dimension_semantics=("parallel", "arbitrary"), vmem_limit_bytes=VMEM_LIMIT),
        name="ssd",
    )(xbc, xbc, z, dt, conv_w, conv_b, alog_row, dskip_row, gnorm_row, expand)


def _outproj_kernel(yT_ref, ys_ref, x_ref, mod_ref, ga_ref, w_ref, gf_ref, x1_ref, h2_ref):
    ya = _rms(yT_ref[0].T) * ga_ref[...]
    ycat = jnp.concatenate([ya.astype(BF16), ys_ref[0]], axis=1)
    x1 = x_ref[0] + mod_ref[0, 2:3, :] * _dot(ycat, w_ref[...])
    x1_ref[0] = x1
    h2 = _rms(x1) * gf_ref[...]
    h2_ref[0] = (h2 * (1.0 + mod_ref[0, 4:5, :]) + mod_ref[0, 3:4, :]).astype(BF16)


def _out_proj(yT, yssd, x, mod, g_att, w_out, g_ffn):
    bsz, s, d = x.shape
    const = lambda b, i: (0, 0)
    return pl.pallas_call(
        _outproj_kernel,
        grid=(bsz, s // TM),
        in_specs=[pl.BlockSpec((1, D_ATT, TM), lambda b, i: (b, 0, i)),
                  pl.BlockSpec((1, TM, D_SSD), lambda b, i: (b, i, 0)),
                  pl.BlockSpec((1, TM, d), lambda b, i: (b, i, 0)),
                  pl.BlockSpec((1, N_MOD, d), lambda b, i: (b, 0, 0)),
                  pl.BlockSpec((1, D_ATT), const),
                  pl.BlockSpec((D_ATT + D_SSD, d), const),
                  pl.BlockSpec((1, d), const)],
        out_specs=(pl.BlockSpec((1, TM, d), lambda b, i: (b, i, 0)),
                   pl.BlockSpec((1, TM, d), lambda b, i: (b, i, 0))),
        out_shape=(jax.ShapeDtypeStruct((bsz, s, d), F32), jax.ShapeDtypeStruct((bsz, s, d), BF16)),
        compiler_params=pltpu.CompilerParams(
            dimension_semantics=("parallel", "parallel"), vmem_limit_bytes=VMEM_LIMIT),
        name="out_proj",
    )(yT, yssd, x, mod, g_att, w_out, g_ffn)


def _ffn_kernel(h_ref, halo_ref, x1_ref, mod_ref, wu_ref, cw_ref, cb_ref, wd_ref, gfin_ref,
                o_ref, bufg_ref, bufv_ref, *, final_norm):
    keep = jnp.where(pl.program_id(1) > 0, 1.0, 0.0).astype(BF16)
    hext = jnp.concatenate([halo_ref[0] * keep, h_ref[0]], axis=0)
    first = BF16_ROWS - FFN_CONV + 1
    acc = jnp.zeros((TM, D_MODEL), F32)
    for j in range(D_FF // TF):
        cols_g = slice(j * TF, (j + 1) * TF)
        cols_v = slice(D_FF + j * TF, D_FF + (j + 1) * TF)
        bufg_ref[...] = _dot(hext, wu_ref[:, cols_g])
        bufv_ref[...] = _dot(hext, wu_ref[:, cols_v])
        ug = cb_ref[:, cols_g]
        uv = cb_ref[:, cols_v]
        for kk in range(FFN_CONV):
            ug = ug + bufg_ref[pl.ds(first + kk, TM), :] * cw_ref[kk:kk + 1, cols_g]
            uv = uv + bufv_ref[pl.ds(first + kk, TM), :] * cw_ref[kk:kk + 1, cols_v]
        act = (ug * _sigmoid(ug) * uv).astype(BF16)
        acc = acc + _dot(act, wd_ref[cols_g, :])
    x2 = x1_ref[0] + mod_ref[0, 5:6, :] * acc
    if final_norm:
        x2 = _rms(x2) * gfin_ref[...]
    o_ref[0] = x2


def _ffn(h2, x1, mod, w_up, conv_w, conv_b, w_down, g_final, final_norm):
    bsz, s, d = x1.shape
    hb = TM // BF16_ROWS
    const = lambda b, i: (0, 0)
    return pl.pallas_call(
        functools.partial(_ffn_kernel, final_norm=final_norm),
        grid=(bsz, s // TM),
        in_specs=[pl.BlockSpec((1, TM, d), lambda b, i: (b, i, 0)),
                  pl.BlockSpec((1, BF16_ROWS, d), lambda b, i: (b, jnp.maximum(i * hb - 1, 0), 0)),
                  pl.BlockSpec((1, TM, d), lambda b, i: (b, i, 0)),
                  pl.BlockSpec((1, N_MOD, d), lambda b, i: (b, 0, 0)),
                  pl.BlockSpec((d, 2 * D_FF), const, pipeline_mode=pl.Buffered(1)),
                  pl.BlockSpec((FFN_CONV, 2 * D_FF), const),
                  pl.BlockSpec((1, 2 * D_FF), const),
                  pl.BlockSpec((D_FF, d), const, pipeline_mode=pl.Buffered(1)),
                  pl.BlockSpec((1, d), const)],
        out_specs=pl.BlockSpec((1, TM, d), lambda b, i: (b, i, 0)),
        out_shape=jax.ShapeDtypeStruct((bsz, s, d), F32),
        scratch_shapes=[pltpu.VMEM((BF16_ROWS + TM, TF), F32),
                        pltpu.VMEM((BF16_ROWS + TM, TF), F32)],
        compiler_params=pltpu.CompilerParams(
            dimension_semantics=("parallel", "parallel"), vmem_limit_bytes=VMEM_LIMIT),
        name="ffn",
    )(h2, h2, x1, mod, w_up, conv_w, conv_b, w_down, g_final)


def _expand_matrix():
    e = np.zeros((LANES, D_SSD), np.float32)
    for hd in range(N_HEADS):
        e[DT_LO + hd, hd * HEAD_DIM:(hd + 1) * HEAD_DIM] = 1.0
    return jnp.asarray(e, BF16)


def _small_row(f_vals, dt_vals):
    row = jnp.zeros((LANES,), F32)
    row = row.at[F_HI:F_HI + N_HEADS].set(f_vals)
    row = row.at[F_MID:F_MID + N_HEADS].set(f_vals)
    row = row.at[F_LO:F_LO + N_HEADS].set(f_vals)
    row = row.at[DT_LO:DT_LO + N_HEADS].set(dt_vals)
    return row.reshape(1, LANES)


def kernel(x, c, mod_w, mod_b, norm_mix_g, norm_ffn_g, w_in, fox_forget_b, attn_norm_g, ssd_conv_w, ssd_conv_b, ssd_dt_bias, ssd_a_log, ssd_d, ssd_norm_g, w_out, ffn_w_up, ffn_conv_w, ffn_conv_b, ffn_w_down, final_g):
    depth = w_in.shape[0]
    bsz = x.shape[0]
    mod_all = _modulation(c, mod_w, mod_b).reshape(depth, bsz, N_MOD, D_MODEL)
    expand = _expand_matrix()
    zeros8 = jnp.zeros((N_HEADS,), F32)

    o_k, o_v, o_f = D_ATT, 2 * D_ATT, 3 * D_ATT
    o_z = o_f + N_HEADS
    o_x = o_z + D_SSD
    o_dt = o_x + D_XBC

    for l in range(depth):
        w = w_in[l]
        w_f = w[:, o_f:o_z]
        w_dt = w[:, o_dt:o_dt + N_HEADS]
        w_small = jnp.zeros((D_MODEL, LANES), F32)
        w_small = w_small.at[:, F_HI:F_HI + N_HEADS].set(w_f).at[:, F_MID:F_MID + N_HEADS].set(w_f)
        w_small = w_small.at[:, F_LO:F_LO + N_HEADS].set(w_f).at[:, DT_LO:DT_LO + N_HEADS].set(w_dt)
        w_rows = jnp.concatenate([w[:, :o_k], w[:, o_v:o_f]], axis=1).T.astype(BF16)
        w_cols = jnp.concatenate([w[:, o_k:o_v], w[:, o_z:o_x], w[:, o_x:o_dt], w_small], axis=1).astype(BF16)
        bias_small = _small_row(fox_forget_b[l], ssd_dt_bias[l])
        mod = mod_all[l]

        qT, vT, k, z, xbc, fparts, dt = _in_proj(
            x, mod, norm_mix_g[l].reshape(1, -1), w_rows, w_cols, bias_small)
        yT = _attention(qT, k, fparts, vT)
        yssd = _ssd(xbc, z, dt, ssd_conv_w[l], ssd_conv_b[l].reshape(1, -1),
                    _small_row(zeros8, ssd_a_log[l]),
                    jnp.repeat(ssd_d[l], HEAD_DIM).reshape(1, -1),
                    ssd_norm_g[l].reshape(1, -1), expand)
        x1, h2 = _out_proj(yT, yssd, x, mod, attn_norm_g[l].reshape(1, -1),
                           w_out[l].astype(BF16), norm_ffn_g[l].reshape(1, -1))
        x = _ffn(h2, x1, mod, ffn_w_up[l].astype(BF16), ffn_conv_w[l], ffn_conv_b[l].reshape(1, -1),
                 ffn_w_down[l].astype(BF16), final_g.reshape(1, -1), final_norm=(l == depth - 1))
    return x
```

```python
import functools

import numpy as np
import jax
import jax.numpy as jnp
from jax import lax
from jax.experimental import pallas as pl
from jax.experimental.pallas import tpu as pltpu

F32 = jnp.float32
BF16 = jnp.bfloat16

D_MODEL = 1024
N_HEADS = 8
HEAD_DIM = 64
D_ATT = 512
D_SSD = 512
SSD_GROUPS = 2
SSD_STATE = 128
SSD_CONV = 4
SSD_CHUNK = 128
D_XBC = D_SSD + 2 * SSD_GROUPS * SSD_STATE
D_FF = 2816
FFN_CONV = 3
N_MOD = 6
EPS = 1e-6
NEG = -1e30
LOG2E = 1.4426950408889634

LANES = 128
BF16_ROWS = 16
VMEM_LIMIT = 56 * 1024 * 1024

TM = 512
TQ = 512
TK = 512
TS = 512
TF = 256

F_HI, DT_LO, F_MID, F_LO = 0, 8, 16, 24


def _sigmoid(x):
    return 1.0 / (1.0 + jnp.exp(-x))


def _split2(x):
    hi = x.astype(BF16)
    lo = (x - hi.astype(F32)).astype(BF16)
    return hi, lo


def _split3(x):
    hi = x.astype(BF16)
    r = x - hi.astype(F32)
    mid = r.astype(BF16)
    lo = (r - mid.astype(F32)).astype(BF16)
    return hi, mid, lo


def _dot(a, b):
    return jnp.dot(a, b, preferred_element_type=F32)


def _dot_nt(a, b):
    return lax.dot_general(a, b, (((1,), (1,)), ((), ())), preferred_element_type=F32)


def _tril(n):
    r = lax.broadcasted_iota(jnp.int32, (n, n), 0)
    c = lax.broadcasted_iota(jnp.int32, (n, n), 1)
    return jnp.where(c <= r, 1.0, 0.0).astype(BF16)


def _rms(x):
    return x * lax.rsqrt(jnp.mean(x * x, axis=-1, keepdims=True) + EPS)


def _mod_kernel(c_ref, w_ref, b_ref, o_ref):
    c = c_ref[...]
    o_ref[0] = _dot(c * _sigmoid(c), w_ref[0]) + b_ref[0]


def _modulation(c, mod_w, mod_b):
    depth, d, n = mod_w.shape
    bsz = c.shape[0]
    tn = 1536
    return pl.pallas_call(
        _mod_kernel,
        grid=(depth, n // tn),
        in_specs=[pl.BlockSpec((bsz, d), lambda l, j: (0, 0)),
                  pl.BlockSpec((1, d, tn), lambda l, j: (l, 0, j)),
                  pl.BlockSpec((1, 1, tn), lambda l, j: (l, 0, j))],
        out_specs=pl.BlockSpec((1, bsz, tn), lambda l, j: (l, 0, j)),
        out_shape=jax.ShapeDtypeStruct((depth, bsz, n), F32),
        compiler_params=pltpu.CompilerParams(
            dimension_semantics=("parallel", "parallel"), vmem_limit_bytes=VMEM_LIMIT),
        name="modulation",
    )(c, mod_w, mod_b.reshape(depth, 1, n))


def _inproj_kernel(x_ref, mod_ref, g_ref, wr_ref, wc_ref, bs_ref,
                   qT_ref, vT_ref, k_ref, z_ref, xbc_ref, fp_ref, dt_ref, carry_ref):
    @pl.when(pl.program_id(1) == 0)
    def _():
        carry_ref[...] = jnp.zeros_like(carry_ref)

    x = x_ref[0]
    y = _rms(x) * g_ref[...]
    h = (y * (1.0 + mod_ref[0, 1:2, :]) + mod_ref[0, 0:1, :]).astype(BF16)

    rows = _dot_nt(wr_ref[...], h)
    qT_ref[0] = (rows[:D_ATT] * (LOG2E * HEAD_DIM ** -0.5)).astype(BF16)
    vT_ref[0, 0] = rows[D_ATT:].astype(BF16)
    k_ref[0] = _dot(h, wc_ref[:, 0:D_ATT]).astype(BF16)
    z_ref[0] = _dot(h, wc_ref[:, D_ATT:D_ATT + D_SSD]).astype(BF16)
    o = D_ATT + D_SSD
    xbc_ref[0] = _dot(h, wc_ref[:, o:o + D_XBC]).astype(BF16)
    small = _dot(h, wc_ref[:, o + D_XBC:o + D_XBC + LANES]) + bs_ref[...]

    t = jnp.log1p(jnp.exp(-jnp.abs(small)))
    dt_ref[0] = jnp.maximum(small, 0.0) + t
    nlf = (jnp.maximum(-small, 0.0) + t) * LOG2E

    tri = _tril(TM)
    hi, mid, lo = _split3(nlf)
    cum = _dot(tri, hi) + _dot(tri, mid) + _dot(tri, lo) + carry_ref[...]
    carry_ref[...] = cum[TM - 1:TM, :]

    hi, mid, lo = _split3(cum)
    lane = lax.broadcasted_iota(jnp.int32, cum.shape, 1) // 8
    parts = jnp.where(lane == F_HI // 8, hi.astype(F32),
                      jnp.where(lane == F_MID // 8, mid.astype(F32),
                                jnp.where(lane == F_LO // 8, lo.astype(F32), 0.0)))
    fp_ref[0] = parts.astype(BF16)


def _in_proj(x, mod, g, w_rows, w_cols, bias_small):
    bsz, s, d = x.shape
    ns = s // TM
    ncol = w_cols.shape[1]
    out_shape = (
        jax.ShapeDtypeStruct((bsz, D_ATT, s), BF16),
        jax.ShapeDtypeStruct((bsz, ns, D_ATT, TM), BF16),
        jax.ShapeDtypeStruct((bsz, s, D_ATT), BF16),
        jax.ShapeDtypeStruct((bsz, s, D_SSD), BF16),
        jax.ShapeDtypeStruct((bsz, s, D_XBC), BF16),
        jax.ShapeDtypeStruct((bsz, s, LANES), BF16),
        jax.ShapeDtypeStruct((bsz, s, LANES), F32),
    )
    return pl.pallas_call(
        _inproj_kernel,
        grid=(bsz, ns),
        in_specs=[pl.BlockSpec((1, TM, d), lambda b, i: (b, i, 0)),
                  pl.BlockSpec((1, N_MOD, d), lambda b, i: (b, 0, 0)),
                  pl.BlockSpec((1, d), lambda b, i: (0, 0)),
                  pl.BlockSpec((2 * D_ATT, d), lambda b, i: (0, 0)),
                  pl.BlockSpec((d, ncol), lambda b, i: (0, 0)),
                  pl.BlockSpec((1, LANES), lambda b, i: (0, 0))],
        out_specs=(pl.BlockSpec((1, D_ATT, TM), lambda b, i: (b, 0, i)),
                   pl.BlockSpec((1, 1, D_ATT, TM), lambda b, i: (b, i, 0, 0)),
                   pl.BlockSpec((1, TM, D_ATT), lambda b, i: (b, i, 0)),
                   pl.BlockSpec((1, TM, D_SSD), lambda b, i: (b, i, 0)),
                   pl.BlockSpec((1, TM, D_XBC), lambda b, i: (b, i, 0)),
                   pl.BlockSpec((1, TM, LANES), lambda b, i: (b, i, 0)),
                   pl.BlockSpec((1, TM, LANES), lambda b, i: (b, i, 0))),
        out_shape=out_shape,
        scratch_shapes=[pltpu.VMEM((1, LANES), F32)],
        compiler_params=pltpu.CompilerParams(
            dimension_semantics=("parallel", "arbitrary"), vmem_limit_bytes=VMEM_LIMIT),
        name="in_proj",
    )(x, mod, g, w_rows, w_cols, bias_small)


def _attn_kernel(qT_ref, k_ref, f_ref, vT_ref, o_ref, s0_ref, s1_ref, acc_ref):
    h = pl.program_id(1)
    odd = h % 2
    nq = qT_ref.shape[2] // TQ
    r = lax.broadcasted_iota(jnp.int32, (LANES, TQ), 0)
    sel = jnp.where(r == h + F_HI, 1.0, jnp.where(r == h + F_MID, 1.0, jnp.where(r == h + F_LO, 1.0, 0.0)))
    sel = sel.astype(BF16)
    ones_rows = jnp.ones((BF16_ROWS, TK), BF16)
    rr = lax.broadcasted_iota(jnp.int32, (TK, TQ), 0)
    cc = lax.broadcasted_iota(jnp.int32, (TK, TQ), 1)
    bufs = (s0_ref, s1_ref)

    def make_qa(i):
        q = qT_ref[0, :, i * TQ:(i + 1) * TQ]
        zq = jnp.zeros_like(q)
        return jnp.concatenate([jnp.where(odd == 0, q, zq), jnp.where(odd == 1, q, zq), sel], axis=0)

    def logits(qa, t, dst):
        ka = jnp.concatenate([k_ref[0, t * TK:(t + 1) * TK, :], f_ref[0, t * TK:(t + 1) * TK, :]], axis=1)
        dst[...] = _dot(ka, qa)

    steps = [(i, t) for i in range(nq) for t in range(i + 1)]
    qa = make_qa(0)
    logits(qa, 0, bufs[0])
    m = None
    for n, (i, t) in enumerate(steps):
        if n + 1 < len(steps):
            i2, t2 = steps[n + 1]
            qa_next = qa if i2 == i else make_qa(i2)
            logits(qa_next, t2, bufs[(n + 1) % 2])
        st = bufs[n % 2][...]
        if t == i:
            st = jnp.where(rr <= cc, st, NEG)
        mx = jnp.max(st, axis=0, keepdims=True)
        m_new = mx if t == 0 else jnp.maximum(m, mx)
        p = jnp.exp2((st - m_new).astype(BF16))
        pv = _dot(jnp.concatenate([vT_ref[0, t], ones_rows], axis=0), p)
        acc = pv if t == 0 else jnp.exp2(m - m_new) * acc_ref[...] + pv
        if t == i:
            o_ref[0, :, i * TQ:(i + 1) * TQ] = acc[:HEAD_DIM] * (1.0 / acc[HEAD_DIM:HEAD_DIM + 1])
        else:
            acc_ref[...] = acc
        m = m_new
        if n + 1 < len(steps):
            qa = qa_next


def _attention(qT, k, fparts, vT):
    bsz, _, s = qT.shape
    nk = s // TK
    return pl.pallas_call(
        _attn_kernel,
        grid=(bsz, N_HEADS),
        in_specs=[pl.BlockSpec((1, HEAD_DIM, s), lambda b, h: (b, h, 0)),
                  pl.BlockSpec((1, s, LANES), lambda b, h: (b, 0, h // 2)),
                  pl.BlockSpec((1, s, LANES), lambda b, h: (b, 0, 0)),
                  pl.BlockSpec((1, nk, HEAD_DIM, TK), lambda b, h: (b, 0, h, 0))],
        out_specs=pl.BlockSpec((1, HEAD_DIM, s), lambda b, h: (b, h, 0)),
        out_shape=jax.ShapeDtypeStruct((bsz, D_ATT, s), F32),
        scratch_shapes=[pltpu.VMEM((TK, TQ), F32), pltpu.VMEM((TK, TQ), F32),
                        pltpu.VMEM((HEAD_DIM + BF16_ROWS, TQ), F32)],
        compiler_params=pltpu.CompilerParams(
            dimension_semantics=("parallel", "parallel"), vmem_limit_bytes=VMEM_LIMIT),
        name="attention",
    )(qT, k, fparts, vT)


def _ssd_kernel(xbc_ref, halo_ref, z_ref, dt_ref, cw_ref, cb_ref, alog_ref, dskip_ref, gn_ref, e_ref,
                y_ref, buf_ref, state_ref):
    s = pl.program_id(1)

    @pl.when(s == 0)
    def _():
        state_ref[...] = jnp.zeros_like(state_ref)

    keep = jnp.where(s > 0, 1.0, 0.0)
    buf_ref[0:BF16_ROWS, :] = halo_ref[0].astype(F32) * keep
    buf_ref[BF16_ROWS:BF16_ROWS + TS, :] = xbc_ref[0].astype(F32)
    conv = cb_ref[...]
    for kk in range(SSD_CONV):
        conv = conv + buf_ref[pl.ds(BF16_ROWS - SSD_CONV + 1 + kk, TS), :] * cw_ref[kk:kk + 1, :]
    xc = conv * _sigmoid(conv)
    xs = xc[:, :D_SSD]
    gn = SSD_GROUPS * SSD_STATE
    bm = xc[:, D_SSD:D_SSD + gn]
    cm = xc[:, D_SSD + gn:]

    lane = lax.broadcasted_iota(jnp.int32, (1, LANES), 1) // 8
    head_lanes = lane == DT_LO // 8
    a_neg = jnp.where(head_lanes, -jnp.exp(alog_ref[...]), 0.0)
    dt_small = jnp.where(head_lanes, dt_ref[0], 0.0)
    a_small = dt_small * a_neg
    e = e_ref[...]
    dhi, dlo = _split2(dt_small)
    dt_exp = _dot(dhi, e) + _dot(dlo, e)

    tri = _tril(SSD_CHUNK)
    rr = lax.broadcasted_iota(jnp.int32, (SSD_CHUNK, SSD_CHUNK), 0)
    cc = lax.broadcasted_iota(jnp.int32, (SSD_CHUNK, SSD_CHUNK), 1)
    causal = cc <= rr
    lane2 = lax.broadcasted_iota(jnp.int32, (SSD_CHUNK, LANES), 1)
    heads_per_group = N_HEADS // SSD_GROUPS
    gw = heads_per_group * HEAD_DIM

    for c in range(TS // SSD_CHUNK):
        rows = slice(c * SSD_CHUNK, (c + 1) * SSD_CHUNK)
        ahi, alo = _split2(a_small[rows])
        cs = _dot(tri, ahi) + _dot(tri, alo)
        chi, clo = _split2(cs)
        cs_exp = _dot(chi, e) + _dot(clo, e)
        cs_t = cs.T
        last = cs_exp[SSD_CHUNK - 1:SSD_CHUNK, :]
        dec_end = jnp.exp(last - cs_exp)
        dec_start = jnp.exp(cs_exp)
        dec_chunk = jnp.exp(last)
        xs_c = xs[rows]
        xdt = xs_c * dt_exp[rows]
        xde = (xdt * dec_end).astype(BF16)
        outs = []
        for g in range(SSD_GROUPS):
            bg = bm[rows, g * SSD_STATE:(g + 1) * SSD_STATE]
            cg = cm[rows, g * SSD_STATE:(g + 1) * SSD_STATE].astype(BF16)
            cb = _dot_nt(cg, bg.astype(BF16))
            st = state_ref[:, g * gw:(g + 1) * gw]
            y_off = _dot(cg, st.astype(BF16)) * dec_start[:, g * gw:(g + 1) * gw]
            pair_out = []
            for pr in range(heads_per_group // 2):
                lo_l = g * gw + pr * LANES
                xp = xdt[:, lo_l:lo_l + LANES]
                acc = None
                for half in range(2):
                    hd = g * heads_per_group + pr * 2 + half
                    col = cs[:, DT_LO + hd:DT_LO + hd + 1]
                    row = cs_t[DT_LO + hd:DT_LO + hd + 1, :]
                    seg = jnp.where(causal, col - row, NEG)
                    mmat = (cb * jnp.exp(seg)).astype(BF16)
                    keep_half = (lane2 < HEAD_DIM) if half == 0 else (lane2 >= HEAD_DIM)
                    xh = jnp.where(keep_half, xp, 0.0).astype(BF16)
                    term = _dot(mmat, xh)
                    acc = term if acc is None else acc + term
                pair_out.append(acc)
            y_diag = jnp.concatenate(pair_out, axis=1)
            outs.append(y_diag + y_off)
            new_state = st * dec_chunk[:, g * gw:(g + 1) * gw] + _dot(bg.T.astype(BF16), xde[:, g * gw:(g + 1) * gw])
            state_ref[:, g * gw:(g + 1) * gw] = new_state
        y = jnp.concatenate(outs, axis=1) + xs_c * dskip_ref[...]
        zc = z_ref[0, rows, :].astype(F32)
        y = y * (zc * _sigmoid(zc))
        normed = [_rms(y[:, g * gw:(g + 1) * gw]) for g in range(SSD_GROUPS)]
        y_ref[0, rows, :] = (jnp.concatenate(normed, axis=1) * gn_ref[...]).astype(BF16)


def _ssd(xbc, z, dt, conv_w, conv_b, alog_row, dskip_row, gnorm_row, expand):
    bsz, s, _ = xbc.shape
    hb = TS // BF16_ROWS
    const = lambda b, i: (0, 0)
    return pl.pallas_call(
        _ssd_kernel,
        grid=(bsz, s // TS),
        in_specs=[pl.BlockSpec((1, TS, D_XBC), lambda b, i: (b, i, 0)),
                  pl.BlockSpec((1, BF16_ROWS, D_XBC), lambda b, i: (b, jnp.maximum(i * hb - 1, 0), 0)),
                  pl.BlockSpec((1, TS, D_SSD), lambda b, i: (b, i, 0)),
                  pl.BlockSpec((1, TS, LANES), lambda b, i: (b, i, 0)),
                  pl.BlockSpec((SSD_CONV, D_XBC), const),
                  pl.BlockSpec((1, D_XBC), const),
                  pl.BlockSpec((1, LANES), const),
                  pl.BlockSpec((1, D_SSD), const),
                  pl.BlockSpec((1, D_SSD), const),
                  pl.BlockSpec((LANES, D_SSD), const)],
        out_specs=pl.BlockSpec((1, TS, D_SSD), lambda b, i: (b, i, 0)),
        out_shape=jax.ShapeDtypeStruct((bsz, s, D_SSD), BF16),
        scratch_shapes=[pltpu.VMEM((BF16_ROWS + TS, D_XBC), F32),
                        pltpu.VMEM((SSD_STATE, D_SSD), F32)],
        compiler_params=pltpu.CompilerParams(
            dimension_semantics=("parallel", "arbitrary"), vmem_limit_bytes=VMEM_LIMIT),
        name="ssd",
    )(xbc, xbc, z, dt, conv_w, conv_b, alog_row, dskip_row, gnorm_row, expand)


def _mlp_kernel(yT_ref, ys_ref, x_ref, mod_ref, ga_ref, wo_ref, gf_ref, wu_ref, cw_ref, cb_ref, wd_ref,
                gfin_ref, o_ref, halo_ref, ga0_ref, va0_ref, ga1_ref, va1_ref, *, final_norm):
    ya = _rms(yT_ref[0].T) * ga_ref[...]
    ycat = jnp.concatenate([ya.astype(BF16), ys_ref[0]], axis=1)
    x1 = x_ref[0] + mod_ref[0, 2:3, :] * _dot(ycat, wo_ref[...])
    h2 = _rms(x1) * gf_ref[...]
    h2 = (h2 * (1.0 + mod_ref[0, 4:5, :]) + mod_ref[0, 3:4, :]).astype(BF16)

    @pl.when(pl.program_id(1) == 0)
    def _():
        halo_ref[...] = jnp.zeros_like(halo_ref)

    hext = jnp.concatenate([halo_ref[...], h2], axis=0)
    halo_ref[...] = h2[TM - BF16_ROWS:, :]

    first = BF16_ROWS - FFN_CONV + 1
    bufs = ((ga0_ref, va0_ref), (ga1_ref, va1_ref))
    n_chunks = D_FF // TF

    def up(j):
        bufg, bufv = bufs[j % 2]
        bufg[...] = _dot(hext, wu_ref[:, j * TF:(j + 1) * TF])
        bufv[...] = _dot(hext, wu_ref[:, D_FF + j * TF:D_FF + (j + 1) * TF])

    up(0)
    acc = jnp.zeros((TM, D_MODEL), F32)
    for j in range(n_chunks):
        if j + 1 < n_chunks:
            up(j + 1)
        bufg, bufv = bufs[j % 2]
        cols_g = slice(j * TF, (j + 1) * TF)
        cols_v = slice(D_FF + j * TF, D_FF + (j + 1) * TF)
        ug = cb_ref[:, cols_g]
        uv = cb_ref[:, cols_v]
        for kk in range(FFN_CONV):
            ug = ug + bufg[pl.ds(first + kk, TM), :] * cw_ref[kk:kk + 1, cols_g]
            uv = uv + bufv[pl.ds(first + kk, TM), :] * cw_ref[kk:kk + 1, cols_v]
        act = (ug * _sigmoid(ug) * uv).astype(BF16)
        acc = acc + _dot(act, wd_ref[cols_g, :])
    x2 = x1 + mod_ref[0, 5:6, :] * acc
    if final_norm:
        x2 = _rms(x2) * gfin_ref[...]
    o_ref[0] = x2


def _mlp(yT, yssd, x, mod, g_att, w_out, g_ffn, w_up, conv_w, conv_b, w_down, g_final, final_norm):
    bsz, s, d = x.shape
    const = lambda b, i: (0, 0)
    resident = dict(pipeline_mode=pl.Buffered(1))
    conv_buf = pltpu.VMEM((BF16_ROWS + TM, TF), F32)
    return pl.pallas_call(
        functools.partial(_mlp_kernel, final_norm=final_norm),
        grid=(bsz, s // TM),
        in_specs=[pl.BlockSpec((1, D_ATT, TM), lambda b, i: (b, 0, i)),
                  pl.BlockSpec((1, TM, D_SSD), lambda b, i: (b, i, 0)),
                  pl.BlockSpec((1, TM, d), lambda b, i: (b, i, 0)),
                  pl.BlockSpec((1, N_MOD, d), lambda b, i: (b, 0, 0)),
                  pl.BlockSpec((1, D_ATT), const),
                  pl.BlockSpec((D_ATT + D_SSD, d), const, **resident),
                  pl.BlockSpec((1, d), const),
                  pl.BlockSpec((d, 2 * D_FF), const, **resident),
                  pl.BlockSpec((FFN_CONV, 2 * D_FF), const),
                  pl.BlockSpec((1, 2 * D_FF), const),
                  pl.BlockSpec((D_FF, d), const, **resident),
                  pl.BlockSpec((1, d), const)],
        out_specs=pl.BlockSpec((1, TM, d), lambda b, i: (b, i, 0)),
        out_shape=jax.ShapeDtypeStruct((bsz, s, d), F32),
        scratch_shapes=[pltpu.VMEM((BF16_ROWS, d), BF16), conv_buf, conv_buf, conv_buf, conv_buf],
        compiler_params=pltpu.CompilerParams(
            dimension_semantics=("parallel", "arbitrary"), vmem_limit_bytes=VMEM_LIMIT),
        name="mlp",
    )(yT, yssd, x, mod, g_att, w_out, g_ffn, w_up, conv_w, conv_b, w_down, g_final)


def _expand_matrix():
    e = np.zeros((LANES, D_SSD), np.float32)
    for hd in range(N_HEADS):
        e[DT_LO + hd, hd * HEAD_DIM:(hd + 1) * HEAD_DIM] = 1.0
    return jnp.asarray(e, BF16)


def _small_row(f_vals, dt_vals):
    row = jnp.zeros((LANES,), F32)
    row = row.at[F_HI:F_HI + N_HEADS].set(f_vals)
    row = row.at[F_MID:F_MID + N_HEADS].set(f_vals)
    row = row.at[F_LO:F_LO + N_HEADS].set(f_vals)
    row = row.at[DT_LO:DT_LO + N_HEADS].set(dt_vals)
    return row.reshape(1, LANES)


def kernel(x, c, mod_w, mod_b, norm_mix_g, norm_ffn_g, w_in, fox_forget_b, attn_norm_g, ssd_conv_w, ssd_conv_b, ssd_dt_bias, ssd_a_log, ssd_d, ssd_norm_g, w_out, ffn_w_up, ffn_conv_w, ffn_conv_b, ffn_w_down, final_g):
    depth = w_in.shape[0]
    bsz = x.shape[0]
    mod_all = _modulation(c, mod_w, mod_b).reshape(depth, bsz, N_MOD, D_MODEL)
    expand = _expand_matrix()
    zeros8 = jnp.zeros((N_HEADS,), F32)

    o_k, o_v, o_f = D_ATT, 2 * D_ATT, 3 * D_ATT
    o_z = o_f + N_HEADS
    o_x = o_z + D_SSD
    o_dt = o_x + D_XBC

    for l in range(depth):
        w = w_in[l]
        w_f = w[:, o_f:o_z]
        w_dt = w[:, o_dt:o_dt + N_HEADS]
        w_small = jnp.zeros((D_MODEL, LANES), F32)
        w_small = w_small.at[:, F_HI:F_HI + N_HEADS].set(w_f).at[:, F_MID:F_MID + N_HEADS].set(w_f)
        w_small = w_small.at[:, F_LO:F_LO + N_HEADS].set(w_f).at[:, DT_LO:DT_LO + N_HEADS].set(w_dt)
        w_rows = jnp.concatenate([w[:, :o_k], w[:, o_v:o_f]], axis=1).T.astype(BF16)
        w_cols = jnp.concatenate([w[:, o_k:o_v], w[:, o_z:o_x], w[:, o_x:o_dt], w_small], axis=1).astype(BF16)
        bias_small = _small_row(fox_forget_b[l], ssd_dt_bias[l])
        mod = mod_all[l]

        qT, vT, k, z, xbc, fparts, dt = _in_proj(
            x, mod, norm_mix_g[l].reshape(1, -1), w_rows, w_cols, bias_small)
        yT = _attention(qT, k, fparts, vT)
        yssd = _ssd(xbc, z, dt, ssd_conv_w[l], ssd_conv_b[l].reshape(1, -1),
                    _small_row(zeros8, ssd_a_log[l]),
                    jnp.repeat(ssd_d[l], HEAD_DIM).reshape(1, -1),
                    ssd_norm_g[l].reshape(1, -1), expand)
        x = _mlp(yT, yssd, x, mod, attn_norm_g[l].reshape(1, -1), w_out[l].astype(BF16),
                 norm_ffn_g[l].reshape(1, -1), ffn_w_up[l].astype(BF16), ffn_conv_w[l],
                 ffn_conv_b[l].reshape(1, -1), ffn_w_down[l].astype(BF16), final_g.reshape(1, -1),
                 final_norm=(l == depth - 1))
    return x
```

```python
import functools

import numpy as np
import jax
import jax.numpy as jnp
from jax import lax
from jax.experimental import pallas as pl
from jax.experimental.pallas import tpu as pltpu

F32 = jnp.float32
BF16 = jnp.bfloat16

D_MODEL = 1024
N_HEADS = 8
HEAD_DIM = 64
D_ATT = 512
D_SSD = 512
SSD_GROUPS = 2
SSD_STATE = 128
SSD_CONV = 4
SSD_CHUNK = 128
D_XBC = D_SSD + 2 * SSD_GROUPS * SSD_STATE
D_FF = 2816
FFN_CONV = 3
N_MOD = 6
EPS = 1e-6
NEG = -1e30
LOG2E = 1.4426950408889634

LANES = 128
BF16_ROWS = 16
VMEM_LIMIT = 56 * 1024 * 1024

TM = 512
TQ = 512
TK = 512
TF = 256

F_HI, DT_LO, F_MID, F_LO = 0, 8, 16, 24


def _sigmoid(x):
    return 1.0 / (1.0 + jnp.exp(-x))


def _split2(x):
    hi = x.astype(BF16)
    lo = (x - hi.astype(F32)).astype(BF16)
    return hi, lo


def _split3(x):
    hi = x.astype(BF16)
    r = x - hi.astype(F32)
    mid = r.astype(BF16)
    lo = (r - mid.astype(F32)).astype(BF16)
    return hi, mid, lo


def _dot(a, b):
    return jnp.dot(a, b, preferred_element_type=F32)


def _dot_nt(a, b):
    return lax.dot_general(a, b, (((1,), (1,)), ((), ())), preferred_element_type=F32)


def _tril(n):
    r = lax.broadcasted_iota(jnp.int32, (n, n), 0)
    c = lax.broadcasted_iota(jnp.int32, (n, n), 1)
    return jnp.where(c <= r, 1.0, 0.0).astype(BF16)


def _rms(x):
    return x * lax.rsqrt(jnp.mean(x * x, axis=-1, keepdims=True) + EPS)


def _mod_kernel(c_ref, w_ref, b_ref, o_ref):
    c = c_ref[...]
    o_ref[0] = _dot(c * _sigmoid(c), w_ref[0]) + b_ref[0]


def _modulation(c, mod_w, mod_b):
    depth, d, n = mod_w.shape
    bsz = c.shape[0]
    tn = 1536
    return pl.pallas_call(
        _mod_kernel,
        grid=(depth, n // tn),
        in_specs=[pl.BlockSpec((bsz, d), lambda l, j: (0, 0)),
                  pl.BlockSpec((1, d, tn), lambda l, j: (l, 0, j)),
                  pl.BlockSpec((1, 1, tn), lambda l, j: (l, 0, j))],
        out_specs=pl.BlockSpec((1, bsz, tn), lambda l, j: (l, 0, j)),
        out_shape=jax.ShapeDtypeStruct((depth, bsz, n), F32),
        compiler_params=pltpu.CompilerParams(
            dimension_semantics=("parallel", "parallel"), vmem_limit_bytes=VMEM_LIMIT),
        name="modulation",
    )(c, mod_w, mod_b.reshape(depth, 1, n))


def _mixer_in_kernel(x_ref, mod_ref, g_ref, wr_ref, wc_ref, bs_ref, cw_ref, cb_ref, alog_ref, dskip_ref,
                     gn_ref, e_ref, qT_ref, vT_ref, k_ref, fp_ref, y_ref, carry_ref, buf_ref, state_ref):
    @pl.when(pl.program_id(1) == 0)
    def _():
        carry_ref[...] = jnp.zeros_like(carry_ref)
        state_ref[...] = jnp.zeros_like(state_ref)
        buf_ref[0:BF16_ROWS, :] = jnp.zeros((BF16_ROWS, D_XBC), F32)

    x = x_ref[0]
    y = _rms(x) * g_ref[...]
    h = (y * (1.0 + mod_ref[0, 1:2, :]) + mod_ref[0, 0:1, :]).astype(BF16)

    o = D_ATT + D_SSD
    buf_ref[BF16_ROWS:BF16_ROWS + TM, :] = _dot(h, wc_ref[:, o:o + D_XBC])
    small = _dot(h, wc_ref[:, o + D_XBC:o + D_XBC + LANES]) + bs_ref[...]

    z = _dot(h, wc_ref[:, D_ATT:D_ATT + D_SSD])

    conv = buf_ref[BF16_ROWS:BF16_ROWS + TM, :] * cw_ref[SSD_CONV - 1:SSD_CONV, :] + cb_ref[...]
    for kk in range(SSD_CONV - 1):
        conv = conv + buf_ref[pl.ds(BF16_ROWS - SSD_CONV + 1 + kk, TM), :] * cw_ref[kk:kk + 1, :]
    buf_ref[0:BF16_ROWS, :] = buf_ref[TM:TM + BF16_ROWS, :]
    xc = conv * _sigmoid(conv)
    xs = xc[:, :D_SSD]
    gn = SSD_GROUPS * SSD_STATE
    bm = xc[:, D_SSD:D_SSD + gn]
    cm = xc[:, D_SSD + gn:]

    t = jnp.log1p(jnp.exp(-jnp.abs(small)))
    dt_all = jnp.maximum(small, 0.0) + t
    nlf = (jnp.maximum(-small, 0.0) + t) * LOG2E

    lane = lax.broadcasted_iota(jnp.int32, (1, LANES), 1) // 8
    head_lanes = lane == DT_LO // 8
    a_neg = jnp.where(head_lanes, -jnp.exp(alog_ref[...]), 0.0)
    dt_small = jnp.where(head_lanes, dt_all, 0.0)
    a_small = dt_small * a_neg
    e = e_ref[...]
    dhi, dlo = _split2(dt_small)
    dt_exp = _dot(dhi, e) + _dot(dlo, e)

    tri = _tril(SSD_CHUNK)
    rr = lax.broadcasted_iota(jnp.int32, (SSD_CHUNK, SSD_CHUNK), 0)
    cc = lax.broadcasted_iota(jnp.int32, (SSD_CHUNK, SSD_CHUNK), 1)
    causal = cc <= rr
    lane2 = lax.broadcasted_iota(jnp.int32, (SSD_CHUNK, LANES), 1)
    heads_per_group = N_HEADS // SSD_GROUPS
    gw = heads_per_group * HEAD_DIM

    def chunk_pre(c):
        rows_c = slice(c * SSD_CHUNK, (c + 1) * SSD_CHUNK)
        ahi, alo = _split2(a_small[rows_c])
        cs = _dot(tri, ahi) + _dot(tri, alo)
        chi, clo = _split2(cs)
        cs_exp = _dot(chi, e) + _dot(clo, e)
        cs_t = cs.T
        cbs, y_offs = [], []
        for g in range(SSD_GROUPS):
            bg = bm[rows_c, g * SSD_STATE:(g + 1) * SSD_STATE]
            cg = cm[rows_c, g * SSD_STATE:(g + 1) * SSD_STATE].astype(BF16)
            cbs.append(_dot_nt(cg, bg.astype(BF16)))
            y_offs.append(_dot(cg, state_ref[:, g * gw:(g + 1) * gw].astype(BF16)))
        return cs, cs_exp, cs_t, cbs, y_offs

    def chunk_main(c, pre):
        cs, cs_exp, cs_t, cbs, y_offs = pre
        rows_c = slice(c * SSD_CHUNK, (c + 1) * SSD_CHUNK)
        last = cs_exp[SSD_CHUNK - 1:SSD_CHUNK, :]
        dec_end = jnp.exp(last - cs_exp)
        dec_start = jnp.exp(cs_exp)
        dec_chunk = jnp.exp(last)
        xs_c = xs[rows_c]
        xdt = xs_c * dt_exp[rows_c]
        xde = (xdt * dec_end).astype(BF16)
        outs = []
        for g in range(SSD_GROUPS):
            bg = bm[rows_c, g * SSD_STATE:(g + 1) * SSD_STATE]
            cb = cbs[g]
            y_off = y_offs[g] * dec_start[:, g * gw:(g + 1) * gw]
            pair_out = []
            for pr in range(heads_per_group // 2):
                lo_l = g * gw + pr * LANES
                xp = xdt[:, lo_l:lo_l + LANES]
                acc = None
                for half in range(2):
                    hd = g * heads_per_group + pr * 2 + half
                    col = cs[:, DT_LO + hd:DT_LO + hd + 1]
                    row = cs_t[DT_LO + hd:DT_LO + hd + 1, :]
                    seg = jnp.where(causal, col - row, NEG)
                    mmat = (cb * jnp.exp(seg)).astype(BF16)
                    keep_half = (lane2 < HEAD_DIM) if half == 0 else (lane2 >= HEAD_DIM)
                    xh = jnp.where(keep_half, xp, 0.0).astype(BF16)
                    term = _dot(mmat, xh)
                    acc = term if acc is None else acc + term
                pair_out.append(acc)
            y_diag = jnp.concatenate(pair_out, axis=1)
            outs.append(y_diag + y_off)
            st = state_ref[:, g * gw:(g + 1) * gw]
            new_state = st * dec_chunk[:, g * gw:(g + 1) * gw] + _dot(bg.T.astype(BF16), xde[:, g * gw:(g + 1) * gw])
            state_ref[:, g * gw:(g + 1) * gw] = new_state
        yc = jnp.concatenate(outs, axis=1) + xs_c * dskip_ref[...]
        zc = z[rows_c]
        yc = yc * (zc * _sigmoid(zc))
        normed = [_rms(yc[:, g * gw:(g + 1) * gw]) for g in range(SSD_GROUPS)]
        y_ref[0, rows_c, :] = (jnp.concatenate(normed, axis=1) * gn_ref[...]).astype(BF16)

    pre = chunk_pre(0)
    qT_ref[0] = (_dot_nt(wr_ref[0:D_ATT, :], h) * (LOG2E * HEAD_DIM ** -0.5)).astype(BF16)
    chunk_main(0, pre)
    pre = chunk_pre(1)
    vT_ref[0, 0] = _dot_nt(wr_ref[D_ATT:2 * D_ATT, :], h).astype(BF16)
    chunk_main(1, pre)
    pre = chunk_pre(2)
    k_ref[0] = _dot(h, wc_ref[:, 0:D_ATT]).astype(BF16)
    chunk_main(2, pre)
    pre = chunk_pre(3)

    tril = _tril(TM)
    hi, mid, lo = _split3(nlf)
    cum = _dot(tril, hi) + _dot(tril, mid) + _dot(tril, lo) + carry_ref[...]
    carry_ref[...] = cum[TM - 1:TM, :]
    chunk_main(3, pre)
    hi, mid, lo = _split3(cum)
    lane_f = lax.broadcasted_iota(jnp.int32, cum.shape, 1) // 8
    parts = jnp.where(lane_f == F_HI // 8, hi.astype(F32),
                      jnp.where(lane_f == F_MID // 8, mid.astype(F32),
                                jnp.where(lane_f == F_LO // 8, lo.astype(F32), 0.0)))
    fp_ref[0] = parts.astype(BF16)


def _layer_spec(shape, l):
    zeros = (0,) * len(shape)
    return pl.BlockSpec((None,) + tuple(shape), lambda b, i: (l,) + zeros)


def _mod_spec(l):
    return pl.BlockSpec((None, 1, N_MOD, D_MODEL), lambda b, i: (l, b, 0, 0))


def _mixer_in(l, x, mod, g, w_rows, w_cols, bias_small, conv_w, conv_b, alog_row, dskip_row, gnorm_row, expand):
    bsz, s, d = x.shape
    ns = s // TM
    ncol = w_cols.shape[-1]
    out_shape = (
        jax.ShapeDtypeStruct((bsz, D_ATT, s), BF16),
        jax.ShapeDtypeStruct((bsz, ns, D_ATT, TM), BF16),
        jax.ShapeDtypeStruct((bsz, s, D_ATT), BF16),
        jax.ShapeDtypeStruct((bsz, s, LANES), BF16),
        jax.ShapeDtypeStruct((bsz, s, D_SSD), BF16),
    )
    return pl.pallas_call(
        _mixer_in_kernel,
        grid=(bsz, ns),
        in_specs=[pl.BlockSpec((1, TM, d), lambda b, i: (b, i, 0)),
                  _mod_spec(l),
                  _layer_spec((1, d), l),
                  _layer_spec((2 * D_ATT, d), l),
                  _layer_spec((d, ncol), l),
                  _layer_spec((1, LANES), l),
                  _layer_spec((SSD_CONV, D_XBC), l),
                  _layer_spec((1, D_XBC), l),
                  _layer_spec((1, LANES), l),
                  _layer_spec((1, D_SSD), l),
                  _layer_spec((1, D_SSD), l),
                  pl.BlockSpec((LANES, D_SSD), lambda b, i: (0, 0))],
        out_specs=(pl.BlockSpec((1, D_ATT, TM), lambda b, i: (b, 0, i)),
                   pl.BlockSpec((1, 1, D_ATT, TM), lambda b, i: (b, i, 0, 0)),
                   pl.BlockSpec((1, TM, D_ATT), lambda b, i: (b, i, 0)),
                   pl.BlockSpec((1, TM, LANES), lambda b, i: (b, i, 0)),
                   pl.BlockSpec((1, TM, D_SSD), lambda b, i: (b, i, 0))),
        out_shape=out_shape,
        scratch_shapes=[pltpu.VMEM((1, LANES), F32),
                        pltpu.VMEM((BF16_ROWS + TM, D_XBC), F32),
                        pltpu.VMEM((SSD_STATE, D_SSD), F32)],
        compiler_params=pltpu.CompilerParams(
            dimension_semantics=("parallel", "arbitrary"), vmem_limit_bytes=VMEM_LIMIT),
        name="mixer_in",
    )(x, mod, g, w_rows, w_cols, bias_small, conv_w, conv_b, alog_row, dskip_row, gnorm_row, expand)


def _attn_kernel(qT_ref, k_ref, f_ref, vT_ref, o_ref, s0_ref, s1_ref, acc_ref):
    h = pl.program_id(1)
    odd = h % 2
    nq = qT_ref.shape[2] // TQ
    r = lax.broadcasted_iota(jnp.int32, (LANES, TQ), 0)
    sel = jnp.where(r == h + F_HI, 1.0, jnp.where(r == h + F_MID, 1.0, jnp.where(r == h + F_LO, 1.0, 0.0)))
    sel = sel.astype(BF16)
    ones_rows = jnp.ones((BF16_ROWS, TK), BF16)
    rr = lax.broadcasted_iota(jnp.int32, (TK, TQ), 0)
    cc = lax.broadcasted_iota(jnp.int32, (TK, TQ), 1)
    bufs = (s0_ref, s1_ref)

    def make_qa(i):
        q = qT_ref[0, :, i * TQ:(i + 1) * TQ]
        zq = jnp.zeros_like(q)
        return jnp.concatenate([jnp.where(odd == 0, q, zq), jnp.where(odd == 1, q, zq), sel], axis=0)

    def logits(qa, t, dst):
        ka = jnp.concatenate([k_ref[0, t * TK:(t + 1) * TK, :], f_ref[0, t * TK:(t + 1) * TK, :]], axis=1)
        dst[...] = _dot(ka, qa)

    steps = [(i, t) for i in range(nq) for t in range(i + 1)]
    qa = make_qa(0)
    logits(qa, 0, bufs[0])
    m = None
    for n, (i, t) in enumerate(steps):
        if n + 1 < len(steps):
            i2, t2 = steps[n + 1]
            qa_next = qa if i2 == i else make_qa(i2)
            logits(qa_next, t2, bufs[(n + 1) % 2])
        st = bufs[n % 2][...]
        if t == i:
            st = jnp.where(rr <= cc, st, NEG)
        mx = jnp.max(st, axis=0, keepdims=True)
        m_new = mx if t == 0 else jnp.maximum(m, mx)
        p = jnp.exp2((st - m_new).astype(BF16))
        pv = _dot(jnp.concatenate([vT_ref[0, t], ones_rows], axis=0), p)
        acc = pv if t == 0 else jnp.exp2(m - m_new) * acc_ref[...] + pv
        if t == i:
            o_ref[0, :, i * TQ:(i + 1) * TQ] = acc[:HEAD_DIM] * (1.0 / acc[HEAD_DIM:HEAD_DIM + 1])
        else:
            acc_ref[...] = acc
        m = m_new
        if n + 1 < len(steps):
            qa = qa_next


def _attention(qT, k, fparts, vT):
    bsz, _, s = qT.shape
    nk = s // TK
    return pl.pallas_call(
        _attn_kernel,
        grid=(bsz, N_HEADS),
        in_specs=[pl.BlockSpec((1, HEAD_DIM, s), lambda b, h: (b, h, 0)),
                  pl.BlockSpec((1, s, LANES), lambda b, h: (b, 0, h // 2)),
                  pl.BlockSpec((1, s, LANES), lambda b, h: (b, 0, 0)),
                  pl.BlockSpec((1, nk, HEAD_DIM, TK), lambda b, h: (b, 0, h, 0))],
        out_specs=pl.BlockSpec((1, HEAD_DIM, s), lambda b, h: (b, h, 0)),
        out_shape=jax.ShapeDtypeStruct((bsz, D_ATT, s), F32),
        scratch_shapes=[pltpu.VMEM((TK, TQ), F32), pltpu.VMEM((TK, TQ), F32),
                        pltpu.VMEM((HEAD_DIM + BF16_ROWS, TQ), F32)],
        compiler_params=pltpu.CompilerParams(
            dimension_semantics=("parallel", "parallel"), vmem_limit_bytes=VMEM_LIMIT),
        name="attention",
    )(qT, k, fparts, vT)


def _mlp_kernel(yT_ref, ys_ref, x_ref, mod_ref, ga_ref, wo_ref, gf_ref, wu_ref, cw_ref, cb_ref, wd_ref,
                gfin_ref, o_ref, halo_ref, ga0_ref, va0_ref, ga1_ref, va1_ref, *, final_norm):
    ya = _rms(yT_ref[0].T) * ga_ref[...]
    ycat = jnp.concatenate([ya.astype(BF16), ys_ref[0]], axis=1)
    x1 = x_ref[0] + mod_ref[0, 2:3, :] * _dot(ycat, wo_ref[...])
    h2 = _rms(x1) * gf_ref[...]
    h2 = (h2 * (1.0 + mod_ref[0, 4:5, :]) + mod_ref[0, 3:4, :]).astype(BF16)

    @pl.when(pl.program_id(1) == 0)
    def _():
        halo_ref[...] = jnp.zeros_like(halo_ref)

    hext = jnp.concatenate([halo_ref[...], h2], axis=0)
    halo_ref[...] = h2[TM - BF16_ROWS:, :]

    first = BF16_ROWS - FFN_CONV + 1
    bufs = ((ga0_ref, va0_ref), (ga1_ref, va1_ref))
    n_chunks = D_FF // TF

    def up(j):
        bufg, bufv = bufs[j % 2]
        bufg[...] = _dot(hext, wu_ref[:, j * TF:(j + 1) * TF])
        bufv[...] = _dot(hext, wu_ref[:, D_FF + j * TF:D_FF + (j + 1) * TF])

    up(0)
    acc = jnp.zeros((TM, D_MODEL), F32)
    for j in range(n_chunks):
        if j + 1 < n_chunks:
            up(j + 1)
        bufg, bufv = bufs[j % 2]
        cols_g = slice(j * TF, (j + 1) * TF)
        cols_v = slice(D_FF + j * TF, D_FF + (j + 1) * TF)
        last = FFN_CONV - 1
        ug = bufg[BF16_ROWS:BF16_ROWS + TM, :] * cw_ref[last:last + 1, cols_g] + cb_ref[:, cols_g]
        uv = bufv[BF16_ROWS:BF16_ROWS + TM, :] * cw_ref[last:last + 1, cols_v] + cb_ref[:, cols_v]
        for kk in range(last):
            ug = ug + bufg[pl.ds(first + kk, TM), :] * cw_ref[kk:kk + 1, cols_g]
            uv = uv + bufv[pl.ds(first + kk, TM), :] * cw_ref[kk:kk + 1, cols_v]
        act = (ug * _sigmoid(ug) * uv).astype(BF16)
        acc = acc + _dot(act, wd_ref[cols_g, :])
    x2 = x1 + mod_ref[0, 5:6, :] * acc
    if final_norm:
        x2 = _rms(x2) * gfin_ref[...]
    o_ref[0] = x2


def _mlp(l, yT, yssd, x, mod, g_att, w_out, g_ffn, w_up, conv_w, conv_b, w_down, g_final, final_norm):
    bsz, s, d = x.shape
    resident = dict(pipeline_mode=pl.Buffered(1))
    conv_buf = pltpu.VMEM((BF16_ROWS + TM, TF), F32)

    def weight_spec(shape):
        return pl.BlockSpec((None,) + shape, lambda b, i: (l, 0, 0), **resident)

    return pl.pallas_call(
        functools.partial(_mlp_kernel, final_norm=final_norm),
        grid=(bsz, s // TM),
        in_specs=[pl.BlockSpec((1, D_ATT, TM), lambda b, i: (b, 0, i)),
                  pl.BlockSpec((1, TM, D_SSD), lambda b, i: (b, i, 0)),
                  pl.BlockSpec((1, TM, d), lambda b, i: (b, i, 0)),
                  _mod_spec(l),
                  _layer_spec((1, D_ATT), l),
                  weight_spec((D_ATT + D_SSD, d)),
                  _layer_spec((1, d), l),
                  weight_spec((d, 2 * D_FF)),
                  _layer_spec((FFN_CONV, 2 * D_FF), l),
                  _layer_spec((1, 2 * D_FF), l),
                  weight_spec((D_FF, d)),
                  pl.BlockSpec((1, d), lambda b, i: (0, 0))],
        out_specs=pl.BlockSpec((1, TM, d), lambda b, i: (b, i, 0)),
        out_shape=jax.ShapeDtypeStruct((bsz, s, d), F32),
        scratch_shapes=[pltpu.VMEM((BF16_ROWS, d), BF16), conv_buf, conv_buf, conv_buf, conv_buf],
        compiler_params=pltpu.CompilerParams(
            dimension_semantics=("parallel", "arbitrary"), vmem_limit_bytes=VMEM_LIMIT),
        name="mlp",
    )(yT, yssd, x, mod, g_att, w_out, g_ffn, w_up, conv_w, conv_b, w_down, g_final)


def _expand_matrix():
    e = np.zeros((LANES, D_SSD), np.float32)
    for hd in range(N_HEADS):
        e[DT_LO + hd, hd * HEAD_DIM:(hd + 1) * HEAD_DIM] = 1.0
    return jnp.asarray(e, BF16)


def _small_rows(f_vals, dt_vals):
    depth = f_vals.shape[0]
    pad = jnp.zeros((depth, LANES - F_LO - N_HEADS), F32)
    return jnp.concatenate([f_vals, dt_vals, f_vals, f_vals, pad], axis=1)[:, None, :]


def kernel(x, c, mod_w, mod_b, norm_mix_g, norm_ffn_g, w_in, fox_forget_b, attn_norm_g, ssd_conv_w, ssd_conv_b, ssd_dt_bias, ssd_a_log, ssd_d, ssd_norm_g, w_out, ffn_w_up, ffn_conv_w, ffn_conv_b, ffn_w_down, final_g):
    depth = w_in.shape[0]
    bsz = x.shape[0]
    mod = _modulation(c, mod_w, mod_b).reshape(depth, bsz, N_MOD, D_MODEL)
    expand = _expand_matrix()

    o_k, o_v, o_f = D_ATT, 2 * D_ATT, 3 * D_ATT
    o_z = o_f + N_HEADS
    o_x = o_z + D_SSD
    o_dt = o_x + D_XBC
    w_f = w_in[:, :, o_f:o_z]
    w_dt = w_in[:, :, o_dt:o_dt + N_HEADS]
    pad = jnp.zeros((depth, D_MODEL, LANES - F_LO - N_HEADS), F32)
    w_rows = jnp.concatenate([w_in[:, :, :o_k], w_in[:, :, o_v:o_f]], axis=2).transpose(0, 2, 1).astype(BF16)
    w_cols = jnp.concatenate([w_in[:, :, o_k:o_v], w_in[:, :, o_z:o_x], w_in[:, :, o_x:o_dt],
                              w_f, w_dt, w_f, w_f, pad], axis=2).astype(BF16)
    bias_small = _small_rows(fox_forget_b, ssd_dt_bias)
    alog_rows = _small_rows(jnp.zeros_like(ssd_a_log), ssd_a_log)
    dskip_rows = jnp.repeat(ssd_d, HEAD_DIM, axis=1)[:, None, :]
    row = lambda a: a[:, None, :]
    w_out_b = w_out.astype(BF16)
    w_up_b = ffn_w_up.astype(BF16)
    w_down_b = ffn_w_down.astype(BF16)
    g_final = final_g.reshape(1, -1)

    for l in range(depth):
        qT, vT, k, fparts, yssd = _mixer_in(
            l, x, mod, row(norm_mix_g), w_rows, w_cols, bias_small, ssd_conv_w, row(ssd_conv_b),
            alog_rows, dskip_rows, row(ssd_norm_g), expand)
        yT = _attention(qT, k, fparts, vT)
        x = _mlp(l, yT, yssd, x, mod, row(attn_norm_g), w_out_b, row(norm_ffn_g), w_up_b, ffn_conv_w,
                 row(ffn_conv_b), w_down_b, g_final, final_norm=(l == depth - 1))
    return x
```

```python
import functools

import numpy as np
import jax
import jax.numpy as jnp
from jax import lax
from jax.experimental import pallas as pl
from jax.experimental.pallas import tpu as pltpu

F32 = jnp.float32
BF16 = jnp.bfloat16

D_MODEL = 1024
N_HEADS = 8
HEAD_DIM = 64
D_ATT = 512
D_SSD = 512
SSD_GROUPS = 2
SSD_STATE = 128
SSD_CONV = 4
SSD_CHUNK = 128
D_XBC = D_SSD + 2 * SSD_GROUPS * SSD_STATE
D_FF = 2816
FFN_CONV = 3
N_MOD = 6
EPS = 1e-6
NEG = -1e30
LOG2E = 1.4426950408889634

LANES = 128
BF16_ROWS = 16
VMEM_LIMIT = 56 * 1024 * 1024

TM = 512
TQ = 512
QH = 256
TK = 512
TF = 256
DOWN_GROUP = 4

F_HI, DT_LO, F_MID, F_LO = 0, 8, 16, 24


def _sigmoid(x):
    return 1.0 / (1.0 + jnp.exp(-x))


def _split2(x):
    hi = x.astype(BF16)
    lo = (x - hi.astype(F32)).astype(BF16)
    return hi, lo


def _split3(x):
    hi = x.astype(BF16)
    r = x - hi.astype(F32)
    mid = r.astype(BF16)
    lo = (r - mid.astype(F32)).astype(BF16)
    return hi, mid, lo


def _dot(a, b):
    return jnp.dot(a, b, preferred_element_type=F32)


def _dot_nt(a, b):
    return lax.dot_general(a, b, (((1,), (1,)), ((), ())), preferred_element_type=F32)


def _tril(n):
    r = lax.broadcasted_iota(jnp.int32, (n, n), 0)
    c = lax.broadcasted_iota(jnp.int32, (n, n), 1)
    return jnp.where(c <= r, 1.0, 0.0).astype(BF16)


def _rms(x):
    return x * lax.rsqrt(jnp.mean(x * x, axis=-1, keepdims=True) + EPS)


def _mod_kernel(c_ref, w_ref, b_ref, o_ref):
    c = c_ref[...]
    o_ref[0] = _dot(c * _sigmoid(c), w_ref[0]) + b_ref[0]


def _modulation(c, mod_w, mod_b):
    depth, d, n = mod_w.shape
    bsz = c.shape[0]
    tn = 1536
    return pl.pallas_call(
        _mod_kernel,
        grid=(depth, n // tn),
        in_specs=[pl.BlockSpec((bsz, d), lambda l, j: (0, 0)),
                  pl.BlockSpec((1, d, tn), lambda l, j: (l, 0, j)),
                  pl.BlockSpec((1, 1, tn), lambda l, j: (l, 0, j))],
        out_specs=pl.BlockSpec((1, bsz, tn), lambda l, j: (l, 0, j)),
        out_shape=jax.ShapeDtypeStruct((depth, bsz, n), F32),
        compiler_params=pltpu.CompilerParams(
            dimension_semantics=("parallel", "parallel"), vmem_limit_bytes=VMEM_LIMIT),
        name="modulation",
    )(c, mod_w, mod_b.reshape(depth, 1, n))


def _mixer_in_kernel(x_ref, mod_ref, g_ref, wr_ref, wc_ref, bs_ref, cw_ref, cb_ref, alog_ref, dskip_ref,
                     gn_ref, e_ref, qT_ref, vT_ref, k_ref, fp_ref, y_ref, carry_ref, buf_ref, state_ref):
    @pl.when(pl.program_id(1) == 0)
    def _():
        carry_ref[...] = jnp.zeros_like(carry_ref)
        state_ref[...] = jnp.zeros_like(state_ref)
        buf_ref[0:BF16_ROWS, :] = jnp.zeros((BF16_ROWS, D_XBC), F32)

    x = x_ref[0]
    y = _rms(x) * g_ref[...]
    h = (y * (1.0 + mod_ref[0, 1:2, :]) + mod_ref[0, 0:1, :]).astype(BF16)

    o = D_ATT + D_SSD
    buf_ref[BF16_ROWS:BF16_ROWS + TM, :] = _dot(h, wc_ref[:, o:o + D_XBC])
    small = _dot(h, wc_ref[:, o + D_XBC:o + D_XBC + LANES]) + bs_ref[...]

    z = _dot(h, wc_ref[:, D_ATT:D_ATT + D_SSD])

    conv = buf_ref[BF16_ROWS:BF16_ROWS + TM, :] * cw_ref[SSD_CONV - 1:SSD_CONV, :] + cb_ref[...]
    for kk in range(SSD_CONV - 1):
        conv = conv + buf_ref[pl.ds(BF16_ROWS - SSD_CONV + 1 + kk, TM), :] * cw_ref[kk:kk + 1, :]
    buf_ref[0:BF16_ROWS, :] = buf_ref[TM:TM + BF16_ROWS, :]
    xc = conv * _sigmoid(conv)
    xs = xc[:, :D_SSD]
    gn = SSD_GROUPS * SSD_STATE
    bm = xc[:, D_SSD:D_SSD + gn]
    cm = xc[:, D_SSD + gn:]

    t = jnp.log1p(jnp.exp(-jnp.abs(small)))
    dt_all = jnp.maximum(small, 0.0) + t
    nlf = (jnp.maximum(-small, 0.0) + t) * LOG2E

    lane = lax.broadcasted_iota(jnp.int32, (1, LANES), 1) // 8
    head_lanes = lane == DT_LO // 8
    a_neg = jnp.where(head_lanes, -jnp.exp(alog_ref[...]), 0.0)
    dt_small = jnp.where(head_lanes, dt_all, 0.0)
    a_small = dt_small * a_neg
    e = e_ref[...]
    dhi, dlo = _split2(dt_small)
    dt_exp = _dot(dhi, e) + _dot(dlo, e)

    tri = _tril(SSD_CHUNK)
    rr = lax.broadcasted_iota(jnp.int32, (SSD_CHUNK, SSD_CHUNK), 0)
    cc = lax.broadcasted_iota(jnp.int32, (SSD_CHUNK, SSD_CHUNK), 1)
    causal = cc <= rr
    lane2 = lax.broadcasted_iota(jnp.int32, (SSD_CHUNK, LANES), 1)
    heads_per_group = N_HEADS // SSD_GROUPS
    gw = heads_per_group * HEAD_DIM

    def chunk_pre(c):
        rows_c = slice(c * SSD_CHUNK, (c + 1) * SSD_CHUNK)
        ahi, alo = _split2(a_small[rows_c])
        cs = _dot(tri, ahi) + _dot(tri, alo)
        chi, clo = _split2(cs)
        cs_exp = _dot(chi, e) + _dot(clo, e)
        cs_t = cs.T
        cbs, y_offs = [], []
        for g in range(SSD_GROUPS):
            bg = bm[rows_c, g * SSD_STATE:(g + 1) * SSD_STATE]
            cg = cm[rows_c, g * SSD_STATE:(g + 1) * SSD_STATE].astype(BF16)
            cbs.append(_dot_nt(cg, bg.astype(BF16)))
            y_offs.append(_dot(cg, state_ref[:, g * gw:(g + 1) * gw].astype(BF16)))
        return cs, cs_exp, cs_t, cbs, y_offs

    def chunk_main(c, pre):
        cs, cs_exp, cs_t, cbs, y_offs = pre
        rows_c = slice(c * SSD_CHUNK, (c + 1) * SSD_CHUNK)
        last = cs_exp[SSD_CHUNK - 1:SSD_CHUNK, :]
        dec_end = jnp.exp(last - cs_exp)
        dec_start = jnp.exp(cs_exp)
        dec_chunk = jnp.exp(last)
        xs_c = xs[rows_c]
        xdt = xs_c * dt_exp[rows_c]
        xde = (xdt * dec_end).astype(BF16)
        outs = []
        for g in range(SSD_GROUPS):
            bg = bm[rows_c, g * SSD_STATE:(g + 1) * SSD_STATE]
            cb = cbs[g]
            y_off = y_offs[g] * dec_start[:, g * gw:(g + 1) * gw]
            pair_out = []
            for pr in range(heads_per_group // 2):
                lo_l = g * gw + pr * LANES
                xp = xdt[:, lo_l:lo_l + LANES]
                acc = None
                for half in range(2):
                    hd = g * heads_per_group + pr * 2 + half
                    col = cs[:, DT_LO + hd:DT_LO + hd + 1]
                    row = cs_t[DT_LO + hd:DT_LO + hd + 1, :]
                    seg = jnp.where(causal, col - row, NEG)
                    mmat = (cb * jnp.exp(seg)).astype(BF16)
                    keep_half = (lane2 < HEAD_DIM) if half == 0 else (lane2 >= HEAD_DIM)
                    xh = jnp.where(keep_half, xp, 0.0).astype(BF16)
                    term = _dot(mmat, xh)
                    acc = term if acc is None else acc + term
                pair_out.append(acc)
            y_diag = jnp.concatenate(pair_out, axis=1)
            outs.append(y_diag + y_off)
            st = state_ref[:, g * gw:(g + 1) * gw]
            new_state = st * dec_chunk[:, g * gw:(g + 1) * gw] + _dot(bg.T.astype(BF16), xde[:, g * gw:(g + 1) * gw])
            state_ref[:, g * gw:(g + 1) * gw] = new_state
        yc = jnp.concatenate(outs, axis=1) + xs_c * dskip_ref[...]
        zc = z[rows_c]
        yc = yc * (zc * _sigmoid(zc))
        normed = [_rms(yc[:, g * gw:(g + 1) * gw]) for g in range(SSD_GROUPS)]
        y_ref[0, rows_c, :] = (jnp.concatenate(normed, axis=1) * gn_ref[...]).astype(BF16)

    pre = chunk_pre(0)
    qT_ref[0] = (_dot_nt(wr_ref[0:D_ATT, :], h) * (LOG2E * HEAD_DIM ** -0.5)).astype(BF16)
    chunk_main(0, pre)
    pre = chunk_pre(1)
    vT_ref[0, 0] = _dot_nt(wr_ref[D_ATT:2 * D_ATT, :], h).astype(BF16)
    chunk_main(1, pre)
    pre = chunk_pre(2)
    k_ref[0] = _dot(h, wc_ref[:, 0:D_ATT]).astype(BF16)
    chunk_main(2, pre)
    pre = chunk_pre(3)

    tril = _tril(TM)
    hi, mid, lo = _split3(nlf)
    cum = _dot(tril, hi) + _dot(tril, mid) + _dot(tril, lo) + carry_ref[...]
    carry_ref[...] = cum[TM - 1:TM, :]
    chunk_main(3, pre)
    hi, mid, lo = _split3(cum)
    lane_f = lax.broadcasted_iota(jnp.int32, cum.shape, 1) // 8
    parts = jnp.where(lane_f == F_HI // 8, hi.astype(F32),
                      jnp.where(lane_f == F_MID // 8, mid.astype(F32),
                                jnp.where(lane_f == F_LO // 8, lo.astype(F32), 0.0)))
    fp_ref[0] = parts.astype(BF16)


def _layer_spec(shape, l):
    zeros = (0,) * len(shape)
    return pl.BlockSpec((None,) + tuple(shape), lambda b, i: (l,) + zeros)


def _mod_spec(l):
    return pl.BlockSpec((None, 1, N_MOD, D_MODEL), lambda b, i: (l, b, 0, 0))


def _mixer_in(l, x, mod, g, w_rows, w_cols, bias_small, conv_w, conv_b, alog_row, dskip_row, gnorm_row, expand):
    bsz, s, d = x.shape
    ns = s // TM
    ncol = w_cols.shape[-1]
    out_shape = (
        jax.ShapeDtypeStruct((bsz, D_ATT, s), BF16),
        jax.ShapeDtypeStruct((bsz, ns, D_ATT, TM), BF16),
        jax.ShapeDtypeStruct((bsz, s, D_ATT), BF16),
        jax.ShapeDtypeStruct((bsz, s, LANES), BF16),
        jax.ShapeDtypeStruct((bsz, s, D_SSD), BF16),
    )
    return pl.pallas_call(
        _mixer_in_kernel,
        grid=(bsz, ns),
        in_specs=[pl.BlockSpec((1, TM, d), lambda b, i: (b, i, 0)),
                  _mod_spec(l),
                  _layer_spec((1, d), l),
                  _layer_spec((2 * D_ATT, d), l),
                  _layer_spec((d, ncol), l),
                  _layer_spec((1, LANES), l),
                  _layer_spec((SSD_CONV, D_XBC), l),
                  _layer_spec((1, D_XBC), l),
                  _layer_spec((1, LANES), l),
                  _layer_spec((1, D_SSD), l),
                  _layer_spec((1, D_SSD), l),
                  pl.BlockSpec((LANES, D_SSD), lambda b, i: (0, 0))],
        out_specs=(pl.BlockSpec((1, D_ATT, TM), lambda b, i: (b, 0, i)),
                   pl.BlockSpec((1, 1, D_ATT, TM), lambda b, i: (b, i, 0, 0)),
                   pl.BlockSpec((1, TM, D_ATT), lambda b, i: (b, i, 0)),
                   pl.BlockSpec((1, TM, LANES), lambda b, i: (b, i, 0)),
                   pl.BlockSpec((1, TM, D_SSD), lambda b, i: (b, i, 0))),
        out_shape=out_shape,
        scratch_shapes=[pltpu.VMEM((1, LANES), F32),
                        pltpu.VMEM((BF16_ROWS + TM, D_XBC), F32),
                        pltpu.VMEM((SSD_STATE, D_SSD), F32)],
        compiler_params=pltpu.CompilerParams(
            dimension_semantics=("parallel", "arbitrary"), vmem_limit_bytes=VMEM_LIMIT),
        name="mixer_in",
    )(x, mod, g, w_rows, w_cols, bias_small, conv_w, conv_b, alog_row, dskip_row, gnorm_row, expand)


def _attn_kernel(qT_ref, k_ref, f_ref, vT_ref, o_ref, s0_ref, s1_ref, acc_ref):
    h = pl.program_id(1)
    odd = h % 2
    nq = qT_ref.shape[2] // TQ
    r = lax.broadcasted_iota(jnp.int32, (LANES, TQ), 0)
    sel = jnp.where(r == h + F_HI, 1.0, jnp.where(r == h + F_MID, 1.0, jnp.where(r == h + F_LO, 1.0, 0.0)))
    sel = sel.astype(BF16)
    ones_rows = jnp.ones((BF16_ROWS, TK), BF16)
    bufs = (s0_ref, s1_ref)

    def make_qa(i):
        q = qT_ref[0, :, i * TQ:(i + 1) * TQ]
        zq = jnp.zeros_like(q)
        return jnp.concatenate([jnp.where(odd == 0, q, zq), jnp.where(odd == 1, q, zq), sel], axis=0)

    def logits(qa, t, dst):
        ka = jnp.concatenate([k_ref[0, t * TK:(t + 1) * TK, :], f_ref[0, t * TK:(t + 1) * TK, :]], axis=1)
        dst[...] = _dot(ka, qa)

    steps = [(i, t) for i in range(nq) for t in range(i + 1)]
    qa = make_qa(0)
    logits(qa, 0, bufs[0])
    m = None
    for n, (i, t) in enumerate(steps):
        if n + 1 < len(steps):
            i2, t2 = steps[n + 1]
            qa_next = qa if i2 == i else make_qa(i2)
            logits(qa_next, t2, bufs[(n + 1) % 2])
        va = jnp.concatenate([vT_ref[0, t], ones_rows], axis=0)
        m_next = []
        for hf in range(TQ // QH):
            cols = slice(hf * QH, (hf + 1) * QH)
            st = bufs[n % 2][:, cols]
            if t == i:
                key = lax.broadcasted_iota(jnp.int32, (TK, QH), 0)
                qry = hf * QH + lax.broadcasted_iota(jnp.int32, (TK, QH), 1)
                st = jnp.where(key <= qry, st, NEG)
            mx = jnp.max(st, axis=0, keepdims=True)
            m_new = mx if t == 0 else jnp.maximum(m[hf], mx)
            p = jnp.exp2((st - m_new).astype(BF16))
            pv = _dot(va, p)
            acc = pv if t == 0 else jnp.exp2(m[hf] - m_new) * acc_ref[:, cols] + pv
            if t == i:
                o_ref[0, :, i * TQ + hf * QH:i * TQ + (hf + 1) * QH] = (
                    acc[:HEAD_DIM] * (1.0 / acc[HEAD_DIM:HEAD_DIM + 1]))
            else:
                acc_ref[:, cols] = acc
            m_next.append(m_new)
        m = m_next
        if n + 1 < len(steps):
            qa = qa_next


def _attention(qT, k, fparts, vT):
    bsz, _, s = qT.shape
    nk = s // TK
    return pl.pallas_call(
        _attn_kernel,
        grid=(bsz, N_HEADS),
        in_specs=[pl.BlockSpec((1, HEAD_DIM, s), lambda b, h: (b, h, 0)),
                  pl.BlockSpec((1, s, LANES), lambda b, h: (b, 0, h // 2)),
                  pl.BlockSpec((1, s, LANES), lambda b, h: (b, 0, 0)),
                  pl.BlockSpec((1, nk, HEAD_DIM, TK), lambda b, h: (b, 0, h, 0))],
        out_specs=pl.BlockSpec((1, HEAD_DIM, s), lambda b, h: (b, h, 0)),
        out_shape=jax.ShapeDtypeStruct((bsz, D_ATT, s), F32),
        scratch_shapes=[pltpu.VMEM((TK, TQ), F32), pltpu.VMEM((TK, TQ), F32),
                        pltpu.VMEM((HEAD_DIM + BF16_ROWS, TQ), F32)],
        compiler_params=pltpu.CompilerParams(
            dimension_semantics=("parallel", "parallel"), vmem_limit_bytes=VMEM_LIMIT),
        name="attention",
    )(qT, k, fparts, vT)


def _mlp_kernel(yT_ref, ys_ref, x_ref, mod_ref, ga_ref, wo_ref, gf_ref, wu_ref, cw_ref, cb_ref, wd_ref,
                gfin_ref, o_ref, halo_ref, ga0_ref, va0_ref, ga1_ref, va1_ref, act_ref, *, final_norm):
    ya = _rms(yT_ref[0].T) * ga_ref[...]
    ycat = jnp.concatenate([ya.astype(BF16), ys_ref[0]], axis=1)
    x1 = x_ref[0] + mod_ref[0, 2:3, :] * _dot(ycat, wo_ref[...])
    h2 = _rms(x1) * gf_ref[...]
    h2 = (h2 * (1.0 + mod_ref[0, 4:5, :]) + mod_ref[0, 3:4, :]).astype(BF16)

    @pl.when(pl.program_id(1) == 0)
    def _():
        halo_ref[...] = jnp.zeros_like(halo_ref)

    hext = jnp.concatenate([halo_ref[...], h2], axis=0)
    halo_ref[...] = h2[TM - BF16_ROWS:, :]

    first = BF16_ROWS - FFN_CONV + 1
    bufs = ((ga0_ref, va0_ref), (ga1_ref, va1_ref))
    n_chunks = D_FF // TF

    def up(j):
        bufg, bufv = bufs[j % 2]
        bufg[...] = _dot(hext, wu_ref[:, j * TF:(j + 1) * TF])
        bufv[...] = _dot(hext, wu_ref[:, D_FF + j * TF:D_FF + (j + 1) * TF])

    up(0)
    acc = None
    for j in range(n_chunks):
        if j + 1 < n_chunks:
            up(j + 1)
        bufg, bufv = bufs[j % 2]
        cols_g = slice(j * TF, (j + 1) * TF)
        cols_v = slice(D_FF + j * TF, D_FF + (j + 1) * TF)
        last = FFN_CONV - 1
        ug = bufg[BF16_ROWS:BF16_ROWS + TM, :] * cw_ref[last:last + 1, cols_g] + cb_ref[:, cols_g]
        uv = bufv[BF16_ROWS:BF16_ROWS + TM, :] * cw_ref[last:last + 1, cols_v] + cb_ref[:, cols_v]
        for kk in range(last):
            ug = ug + bufg[pl.ds(first + kk, TM), :] * cw_ref[kk:kk + 1, cols_g]
            uv = uv + bufv[pl.ds(first + kk, TM), :] * cw_ref[kk:kk + 1, cols_v]
        act_ref[:, cols_g] = (ug * _sigmoid(ug) * uv).astype(BF16)
        if (j + 1) % DOWN_GROUP == 0 or j + 1 == n_chunks:
            lo = (j // DOWN_GROUP) * DOWN_GROUP * TF
            part = _dot(act_ref[:, lo:(j + 1) * TF], wd_ref[lo:(j + 1) * TF, :])
            acc = part if acc is None else acc + part
    x2 = x1 + mod_ref[0, 5:6, :] * acc
    if final_norm:
        x2 = _rms(x2) * gfin_ref[...]
    o_ref[0] = x2


def _mlp(l, yT, yssd, x, mod, g_att, w_out, g_ffn, w_up, conv_w, conv_b, w_down, g_final, final_norm):
    bsz, s, d = x.shape
    resident = dict(pipeline_mode=pl.Buffered(1))
    conv_buf = pltpu.VMEM((BF16_ROWS + TM, TF), F32)

    def weight_spec(shape):
        return pl.BlockSpec((None,) + shape, lambda b, i: (l, 0, 0), **resident)

    return pl.pallas_call(
        functools.partial(_mlp_kernel, final_norm=final_norm),
        grid=(bsz, s // TM),
        in_specs=[pl.BlockSpec((1, D_ATT, TM), lambda b, i: (b, 0, i)),
                  pl.BlockSpec((1, TM, D_SSD), lambda b, i: (b, i, 0)),
                  pl.BlockSpec((1, TM, d), lambda b, i: (b, i, 0)),
                  _mod_spec(l),
                  _layer_spec((1, D_ATT), l),
                  weight_spec((D_ATT + D_SSD, d)),
                  _layer_spec((1, d), l),
                  weight_spec((d, 2 * D_FF)),
                  _layer_spec((FFN_CONV, 2 * D_FF), l),
                  _layer_spec((1, 2 * D_FF), l),
                  weight_spec((D_FF, d)),
                  pl.BlockSpec((1, d), lambda b, i: (0, 0))],
        out_specs=pl.BlockSpec((1, TM, d), lambda b, i: (b, i, 0)),
        out_shape=jax.ShapeDtypeStruct((bsz, s, d), F32),
        scratch_shapes=[pltpu.VMEM((BF16_ROWS, d), BF16), conv_buf, conv_buf, conv_buf, conv_buf,
                        pltpu.VMEM((TM, D_FF), BF16)],
        compiler_params=pltpu.CompilerParams(
            dimension_semantics=("parallel", "arbitrary"), vmem_limit_bytes=VMEM_LIMIT),
        name="mlp",
    )(yT, yssd, x, mod, g_att, w_out, g_ffn, w_up, conv_w, conv_b, w_down, g_final)


def _expand_matrix():
    e = np.zeros((LANES, D_SSD), np.float32)
    for hd in range(N_HEADS):
        e[DT_LO + hd, hd * HEAD_DIM:(hd + 1) * HEAD_DIM] = 1.0
    return jnp.asarray(e, BF16)


def _small_rows(f_vals, dt_vals):
    depth = f_vals.shape[0]
    pad = jnp.zeros((depth, LANES - F_LO - N_HEADS), F32)
    return jnp.concatenate([f_vals, dt_vals, f_vals, f_vals, pad], axis=1)[:, None, :]


def kernel(x, c, mod_w, mod_b, norm_mix_g, norm_ffn_g, w_in, fox_forget_b, attn_norm_g, ssd_conv_w, ssd_conv_b, ssd_dt_bias, ssd_a_log, ssd_d, ssd_norm_g, w_out, ffn_w_up, ffn_conv_w, ffn_conv_b, ffn_w_down, final_g):
    depth = w_in.shape[0]
    bsz = x.shape[0]
    mod = _modulation(c, mod_w, mod_b).reshape(depth, bsz, N_MOD, D_MODEL)
    expand = _expand_matrix()

    o_k, o_v, o_f = D_ATT, 2 * D_ATT, 3 * D_ATT
    o_z = o_f + N_HEADS
    o_x = o_z + D_SSD
    o_dt = o_x + D_XBC
    w_f = w_in[:, :, o_f:o_z]
    w_dt = w_in[:, :, o_dt:o_dt + N_HEADS]
    pad = jnp.zeros((depth, D_MODEL, LANES - F_LO - N_HEADS), F32)
    w_rows = jnp.concatenate([w_in[:, :, :o_k], w_in[:, :, o_v:o_f]], axis=2).transpose(0, 2, 1).astype(BF16)
    w_cols = jnp.concatenate([w_in[:, :, o_k:o_v], w_in[:, :, o_z:o_x], w_in[:, :, o_x:o_dt],
                              w_f, w_dt, w_f, w_f, pad], axis=2).astype(BF16)
    bias_small = _small_rows(fox_forget_b, ssd_dt_bias)
    alog_rows = _small_rows(jnp.zeros_like(ssd_a_log), ssd_a_log)
    dskip_rows = jnp.repeat(ssd_d, HEAD_DIM, axis=1)[:, None, :]
    row = lambda a: a[:, None, :]
    w_out_b = w_out.astype(BF16)
    w_up_b = ffn_w_up.astype(BF16)
    w_down_b = ffn_w_down.astype(BF16)
    g_final = final_g.reshape(1, -1)

    for l in range(depth):
        qT, vT, k, fparts, yssd = _mixer_in(
            l, x, mod, row(norm_mix_g), w_rows, w_cols, bias_small, ssd_conv_w, row(ssd_conv_b),
            alog_rows, dskip_rows, row(ssd_norm_g), expand)
        yT = _attention(qT, k, fparts, vT)
        x = _mlp(l, yT, yssd, x, mod, row(attn_norm_g), w_out_b, row(norm_ffn_g), w_up_b, ffn_conv_w,
                 row(ffn_conv_b), w_down_b, g_final, final_norm=(l == depth - 1))
    return x
```

```python
import functools

import numpy as np
import jax
import jax.numpy as jnp
from jax import lax
from jax.experimental import pallas as pl
from jax.experimental.pallas import tpu as pltpu

F32 = jnp.float32
BF16 = jnp.bfloat16

D_MODEL = 1024
N_HEADS = 8
HEAD_DIM = 64
D_ATT = 512
D_SSD = 512
SSD_GROUPS = 2
SSD_STATE = 128
SSD_CONV = 4
SSD_CHUNK = 128
D_XBC = D_SSD + 2 * SSD_GROUPS * SSD_STATE
D_FF = 2816
FFN_CONV = 3
N_MOD = 6
EPS = 1e-6
NEG = -1e30
LOG2E = 1.4426950408889634

LANES = 128
BF16_ROWS = 16
VMEM_LIMIT = 56 * 1024 * 1024

TM = 512
TQ = 512
TK = 512
TF = 256
DOWN_GROUP = 6

F_HI, DT_LO, F_MID, F_LO = 0, 8, 16, 24


def _sigmoid(x):
    return 1.0 / (1.0 + jnp.exp(-x))


def _split2(x):
    hi = x.astype(BF16)
    lo = (x - hi.astype(F32)).astype(BF16)
    return hi, lo


def _split3(x):
    hi = x.astype(BF16)
    r = x - hi.astype(F32)
    mid = r.astype(BF16)
    lo = (r - mid.astype(F32)).astype(BF16)
    return hi, mid, lo


def _dot(a, b):
    return jnp.dot(a, b, preferred_element_type=F32)


def _dot_nt(a, b):
    return lax.dot_general(a, b, (((1,), (1,)), ((), ())), preferred_element_type=F32)


def _tril(n):
    r = lax.broadcasted_iota(jnp.int32, (n, n), 0)
    c = lax.broadcasted_iota(jnp.int32, (n, n), 1)
    return jnp.where(c <= r, 1.0, 0.0).astype(BF16)


def _rms(x):
    return x * lax.rsqrt(jnp.mean(x * x, axis=-1, keepdims=True) + EPS)


def _mod_kernel(c_ref, w_ref, b_ref, o_ref):
    c = c_ref[...]
    o_ref[0] = _dot(c * _sigmoid(c), w_ref[0]) + b_ref[0]


def _modulation(c, mod_w, mod_b):
    depth, d, n = mod_w.shape
    bsz = c.shape[0]
    tn = 1536
    return pl.pallas_call(
        _mod_kernel,
        grid=(depth, n // tn),
        in_specs=[pl.BlockSpec((bsz, d), lambda l, j: (0, 0)),
                  pl.BlockSpec((1, d, tn), lambda l, j: (l, 0, j)),
                  pl.BlockSpec((1, 1, tn), lambda l, j: (l, 0, j))],
        out_specs=pl.BlockSpec((1, bsz, tn), lambda l, j: (l, 0, j)),
        out_shape=jax.ShapeDtypeStruct((depth, bsz, n), F32),
        compiler_params=pltpu.CompilerParams(
            dimension_semantics=("parallel", "parallel"), vmem_limit_bytes=VMEM_LIMIT),
        name="modulation",
    )(c, mod_w, mod_b.reshape(depth, 1, n))


def _mixer_in_kernel(x_ref, mod_ref, g_ref, wr_ref, wc_ref, bs_ref, cw_ref, cb_ref, alog_ref, dskip_ref,
                     gn_ref, e_ref, qT_ref, vT_ref, k_ref, fp_ref, y_ref, carry_ref, buf_ref, state_ref):
    @pl.when(pl.program_id(1) == 0)
    def _():
        carry_ref[...] = jnp.zeros_like(carry_ref)
        state_ref[...] = jnp.zeros_like(state_ref)
        buf_ref[0:BF16_ROWS, :] = jnp.zeros((BF16_ROWS, D_XBC), F32)

    x = x_ref[0]
    y = _rms(x) * g_ref[...]
    h = (y * (1.0 + mod_ref[0, 1:2, :]) + mod_ref[0, 0:1, :]).astype(BF16)

    o = D_ATT + D_SSD
    buf_ref[BF16_ROWS:BF16_ROWS + TM, :] = _dot(h, wc_ref[:, o:o + D_XBC])
    small = _dot(h, wc_ref[:, o + D_XBC:o + D_XBC + LANES]) + bs_ref[...]

    z = _dot(h, wc_ref[:, D_ATT:D_ATT + D_SSD])

    conv = buf_ref[BF16_ROWS:BF16_ROWS + TM, :] * cw_ref[SSD_CONV - 1:SSD_CONV, :] + cb_ref[...]
    for kk in range(SSD_CONV - 1):
        conv = conv + buf_ref[pl.ds(BF16_ROWS - SSD_CONV + 1 + kk, TM), :] * cw_ref[kk:kk + 1, :]
    buf_ref[0:BF16_ROWS, :] = buf_ref[TM:TM + BF16_ROWS, :]
    xc = conv * _sigmoid(conv)
    xs = xc[:, :D_SSD]
    gn = SSD_GROUPS * SSD_STATE
    bm = xc[:, D_SSD:D_SSD + gn]
    cm = xc[:, D_SSD + gn:]

    t = jnp.log1p(jnp.exp(-jnp.abs(small)))
    dt_all = jnp.maximum(small, 0.0) + t
    nlf = (jnp.maximum(-small, 0.0) + t) * LOG2E

    lane = lax.broadcasted_iota(jnp.int32, (1, LANES), 1) // 8
    head_lanes = lane == DT_LO // 8
    a_neg = jnp.where(head_lanes, -jnp.exp(alog_ref[...]), 0.0)
    dt_small = jnp.where(head_lanes, dt_all, 0.0)
    a_small = dt_small * a_neg
    e = e_ref[...]
    dhi, dlo = _split2(dt_small)
    dt_exp = _dot(dhi, e) + _dot(dlo, e)

    tri = _tril(SSD_CHUNK)
    rr = lax.broadcasted_iota(jnp.int32, (SSD_CHUNK, SSD_CHUNK), 0)
    cc = lax.broadcasted_iota(jnp.int32, (SSD_CHUNK, SSD_CHUNK), 1)
    causal = cc <= rr
    lane2 = lax.broadcasted_iota(jnp.int32, (SSD_CHUNK, LANES), 1)
    heads_per_group = N_HEADS // SSD_GROUPS
    gw = heads_per_group * HEAD_DIM

    def chunk_pre(c):
        rows_c = slice(c * SSD_CHUNK, (c + 1) * SSD_CHUNK)
        ahi, alo = _split2(a_small[rows_c])
        cs = _dot(tri, ahi) + _dot(tri, alo)
        chi, clo = _split2(cs)
        cs_exp = _dot(chi, e) + _dot(clo, e)
        cs_t = cs.T
        cbs, y_offs = [], []
        for g in range(SSD_GROUPS):
            bg = bm[rows_c, g * SSD_STATE:(g + 1) * SSD_STATE]
            cg = cm[rows_c, g * SSD_STATE:(g + 1) * SSD_STATE].astype(BF16)
            cbs.append(_dot_nt(cg, bg.astype(BF16)))
            y_offs.append(_dot(cg, state_ref[:, g * gw:(g + 1) * gw].astype(BF16)))
        return cs, cs_exp, cs_t, cbs, y_offs

    def chunk_main(c, pre):
        cs, cs_exp, cs_t, cbs, y_offs = pre
        rows_c = slice(c * SSD_CHUNK, (c + 1) * SSD_CHUNK)
        last = cs_exp[SSD_CHUNK - 1:SSD_CHUNK, :]
        dec_end = jnp.exp(last - cs_exp)
        dec_start = jnp.exp(cs_exp)
        dec_chunk = jnp.exp(last)
        xs_c = xs[rows_c]
        xdt = xs_c * dt_exp[rows_c]
        xde = (xdt * dec_end).astype(BF16)
        outs = []
        for g in range(SSD_GROUPS):
            bg = bm[rows_c, g * SSD_STATE:(g + 1) * SSD_STATE]
            cb = cbs[g]
            y_off = y_offs[g] * dec_start[:, g * gw:(g + 1) * gw]
            pair_out = []
            for pr in range(heads_per_group // 2):
                lo_l = g * gw + pr * LANES
                xp = xdt[:, lo_l:lo_l + LANES]
                acc = None
                for half in range(2):
                    hd = g * heads_per_group + pr * 2 + half
                    col = cs[:, DT_LO + hd:DT_LO + hd + 1]
                    row = cs_t[DT_LO + hd:DT_LO + hd + 1, :]
                    seg = jnp.where(causal, col - row, NEG)
                    mmat = (cb * jnp.exp(seg)).astype(BF16)
                    keep_half = (lane2 < HEAD_DIM) if half == 0 else (lane2 >= HEAD_DIM)
                    xh = jnp.where(keep_half, xp, 0.0).astype(BF16)
                    term = _dot(mmat, xh)
                    acc = term if acc is None else acc + term
                pair_out.append(acc)
            y_diag = jnp.concatenate(pair_out, axis=1)
            outs.append(y_diag + y_off)
            st = state_ref[:, g * gw:(g + 1) * gw]
            new_state = st * dec_chunk[:, g * gw:(g + 1) * gw] + _dot(bg.T.astype(BF16), xde[:, g * gw:(g + 1) * gw])
            state_ref[:, g * gw:(g + 1) * gw] = new_state
        yc = jnp.concatenate(outs, axis=1) + xs_c * dskip_ref[...]
        zc = z[rows_c]
        yc = yc * (zc * _sigmoid(zc))
        normed = [_rms(yc[:, g * gw:(g + 1) * gw]) for g in range(SSD_GROUPS)]
        y_ref[0, rows_c, :] = (jnp.concatenate(normed, axis=1) * gn_ref[...]).astype(BF16)

    pre = chunk_pre(0)
    qT_ref[0] = (_dot_nt(wr_ref[0:D_ATT, :], h) * (LOG2E * HEAD_DIM ** -0.5)).astype(BF16)
    chunk_main(0, pre)
    pre = chunk_pre(1)
    vT_ref[0, 0] = _dot_nt(wr_ref[D_ATT:2 * D_ATT, :], h).astype(BF16)
    chunk_main(1, pre)
    pre = chunk_pre(2)
    k_ref[0] = _dot(h, wc_ref[:, 0:D_ATT]).astype(BF16)
    chunk_main(2, pre)
    pre = chunk_pre(3)

    tril = _tril(TM)
    hi, mid, lo = _split3(nlf)
    cum = _dot(tril, hi) + _dot(tril, mid) + _dot(tril, lo) + carry_ref[...]
    carry_ref[...] = cum[TM - 1:TM, :]
    chunk_main(3, pre)
    hi, mid, lo = _split3(cum)
    lane_f = lax.broadcasted_iota(jnp.int32, cum.shape, 1) // 8
    parts = jnp.where(lane_f == F_HI // 8, hi.astype(F32),
                      jnp.where(lane_f == F_MID // 8, mid.astype(F32),
                                jnp.where(lane_f == F_LO // 8, lo.astype(F32), 0.0)))
    fp_ref[0] = parts.astype(BF16)


def _layer_spec(shape, l):
    zeros = (0,) * len(shape)
    return pl.BlockSpec((None,) + tuple(shape), lambda b, i: (l,) + zeros)


def _mod_spec(l):
    return pl.BlockSpec((None, 1, N_MOD, D_MODEL), lambda b, i: (l, b, 0, 0))


def _mixer_in(l, x, mod, g, w_rows, w_cols, bias_small, conv_w, conv_b, alog_row, dskip_row, gnorm_row, expand):
    bsz, s, d = x.shape
    ns = s // TM
    ncol = w_cols.shape[-1]
    out_shape = (
        jax.ShapeDtypeStruct((bsz, D_ATT, s), BF16),
        jax.ShapeDtypeStruct((bsz, ns, D_ATT, TM), BF16),
        jax.ShapeDtypeStruct((bsz, s, D_ATT), BF16),
        jax.ShapeDtypeStruct((bsz, s, LANES), BF16),
        jax.ShapeDtypeStruct((bsz, s, D_SSD), BF16),
    )
    return pl.pallas_call(
        _mixer_in_kernel,
        grid=(bsz, ns),
        in_specs=[pl.BlockSpec((1, TM, d), lambda b, i: (b, i, 0)),
                  _mod_spec(l),
                  _layer_spec((1, d), l),
                  _layer_spec((2 * D_ATT, d), l),
                  _layer_spec((d, ncol), l),
                  _layer_spec((1, LANES), l),
                  _layer_spec((SSD_CONV, D_XBC), l),
                  _layer_spec((1, D_XBC), l),
                  _layer_spec((1, LANES), l),
                  _layer_spec((1, D_SSD), l),
                  _layer_spec((1, D_SSD), l),
                  pl.BlockSpec((LANES, D_SSD), lambda b, i: (0, 0))],
        out_specs=(pl.BlockSpec((1, D_ATT, TM), lambda b, i: (b, 0, i)),
                   pl.BlockSpec((1, 1, D_ATT, TM), lambda b, i: (b, i, 0, 0)),
                   pl.BlockSpec((1, TM, D_ATT), lambda b, i: (b, i, 0)),
                   pl.BlockSpec((1, TM, LANES), lambda b, i: (b, i, 0)),
                   pl.BlockSpec((1, TM, D_SSD), lambda b, i: (b, i, 0))),
        out_shape=out_shape,
        scratch_shapes=[pltpu.VMEM((1, LANES), F32),
                        pltpu.VMEM((BF16_ROWS + TM, D_XBC), F32),
                        pltpu.VMEM((SSD_STATE, D_SSD), F32)],
        compiler_params=pltpu.CompilerParams(
            dimension_semantics=("parallel", "arbitrary"), vmem_limit_bytes=VMEM_LIMIT),
        name="mixer_in",
    )(x, mod, g, w_rows, w_cols, bias_small, conv_w, conv_b, alog_row, dskip_row, gnorm_row, expand)


def _attn_kernel(qT_ref, k_ref, f_ref, vT_ref, o_ref, s0_ref, s1_ref, acc_ref):
    h = pl.program_id(1)
    odd = h % 2
    nq = qT_ref.shape[2] // TQ
    r = lax.broadcasted_iota(jnp.int32, (LANES, TQ), 0)
    sel = jnp.where(r == h + F_HI, 1.0, jnp.where(r == h + F_MID, 1.0, jnp.where(r == h + F_LO, 1.0, 0.0)))
    sel = sel.astype(BF16)
    ones_rows = jnp.ones((BF16_ROWS, TK), BF16)
    bufs = (s0_ref, s1_ref)

    def make_qa(i):
        q = qT_ref[0, :, i * TQ:(i + 1) * TQ]
        zq = jnp.zeros_like(q)
        return jnp.concatenate([jnp.where(odd == 0, q, zq), jnp.where(odd == 1, q, zq), sel], axis=0)

    hk = TK // 2

    def logits(qa, i, t, dst):
        ka = jnp.concatenate([k_ref[0, t * TK:(t + 1) * TK, :], f_ref[0, t * TK:(t + 1) * TK, :]], axis=1)
        if t < i:
            dst[...] = _dot(ka, qa)
        else:
            dst[0:hk, :] = _dot(ka[0:hk], qa)
            dst[hk:, hk:] = _dot(ka[hk:], qa[:, hk:])

    def below_diagonal(rows, cols):
        key = lax.broadcasted_iota(jnp.int32, (rows, cols), 0)
        qry = lax.broadcasted_iota(jnp.int32, (rows, cols), 1)
        return key <= qry

    steps = [(i, t) for i in range(nq) for t in range(i + 1)]
    qa = make_qa(0)
    logits(qa, 0, 0, bufs[0])
    m = None
    for n, (i, t) in enumerate(steps):
        if n + 1 < len(steps):
            i2, t2 = steps[n + 1]
            qa_next = qa if i2 == i else make_qa(i2)
            logits(qa_next, i2, t2, bufs[(n + 1) % 2])
        va = jnp.concatenate([vT_ref[0, t], ones_rows], axis=0)
        if t < i:
            st = bufs[n % 2][...]
            mx = jnp.max(st, axis=0, keepdims=True)
            m_new = mx if t == 0 else jnp.maximum(m, mx)
            pv = _dot(va, jnp.exp2((st - m_new).astype(BF16)))
        else:
            top = jnp.where(below_diagonal(hk, TQ), bufs[n % 2][0:hk, :], NEG)
            bot = jnp.where(below_diagonal(hk, hk), bufs[n % 2][hk:, hk:], NEG)
            mx = jnp.max(top, axis=0, keepdims=True)
            mx = jnp.concatenate(
                [mx[:, :hk], jnp.maximum(mx[:, hk:], jnp.max(bot, axis=0, keepdims=True))], axis=1)
            m_new = mx if t == 0 else jnp.maximum(m, mx)
            pv = _dot(va[:, 0:hk], jnp.exp2((top - m_new).astype(BF16)))
            pv_bot = _dot(va[:, hk:], jnp.exp2((bot - m_new[:, hk:]).astype(BF16)))
            pv = jnp.concatenate([pv[:, :hk], pv[:, hk:] + pv_bot], axis=1)
        acc = pv if t == 0 else jnp.exp2(m - m_new) * acc_ref[...] + pv
        if t == i:
            o_ref[0, :, i * TQ:(i + 1) * TQ] = acc[:HEAD_DIM] * (1.0 / acc[HEAD_DIM:HEAD_DIM + 1])
        else:
            acc_ref[...] = acc
        m = m_new
        if n + 1 < len(steps):
            qa = qa_next


def _attention(qT, k, fparts, vT):
    bsz, _, s = qT.shape
    nk = s // TK
    return pl.pallas_call(
        _attn_kernel,
        grid=(bsz, N_HEADS),
        in_specs=[pl.BlockSpec((1, HEAD_DIM, s), lambda b, h: (b, h, 0)),
                  pl.BlockSpec((1, s, LANES), lambda b, h: (b, 0, h // 2)),
                  pl.BlockSpec((1, s, LANES), lambda b, h: (b, 0, 0)),
                  pl.BlockSpec((1, nk, HEAD_DIM, TK), lambda b, h: (b, 0, h, 0))],
        out_specs=pl.BlockSpec((1, HEAD_DIM, s), lambda b, h: (b, h, 0)),
        out_shape=jax.ShapeDtypeStruct((bsz, D_ATT, s), F32),
        scratch_shapes=[pltpu.VMEM((TK, TQ), F32), pltpu.VMEM((TK, TQ), F32),
                        pltpu.VMEM((HEAD_DIM + BF16_ROWS, TQ), F32)],
        compiler_params=pltpu.CompilerParams(
            dimension_semantics=("parallel", "parallel"), vmem_limit_bytes=VMEM_LIMIT),
        name="attention",
    )(qT, k, fparts, vT)


def _mlp_kernel(yT_ref, ys_ref, x_ref, mod_ref, ga_ref, wo_ref, gf_ref, wu_ref, cw_ref, cb_ref, wd_ref,
                gfin_ref, o_ref, halo_ref, ga0_ref, va0_ref, ga1_ref, va1_ref, act_ref, *, final_norm):
    ya = _rms(yT_ref[0].T) * ga_ref[...]
    ycat = jnp.concatenate([ya.astype(BF16), ys_ref[0]], axis=1)
    x1 = x_ref[0] + mod_ref[0, 2:3, :] * _dot(ycat, wo_ref[...])
    h2 = _rms(x1) * gf_ref[...]
    h2 = (h2 * (1.0 + mod_ref[0, 4:5, :]) + mod_ref[0, 3:4, :]).astype(BF16)

    @pl.when(pl.program_id(1) == 0)
    def _():
        halo_ref[...] = jnp.zeros_like(halo_ref)

    hext = jnp.concatenate([halo_ref[...], h2], axis=0)
    halo_ref[...] = h2[TM - BF16_ROWS:, :]

    first = BF16_ROWS - FFN_CONV + 1
    bufs = ((ga0_ref, va0_ref), (ga1_ref, va1_ref))
    n_chunks = D_FF // TF

    def up(j):
        bufg, bufv = bufs[j % 2]
        bufg[...] = _dot(hext, wu_ref[:, j * TF:(j + 1) * TF])
        bufv[...] = _dot(hext, wu_ref[:, D_FF + j * TF:D_FF + (j + 1) * TF])

    up(0)
    acc = None
    for j in range(n_chunks):
        if j + 1 < n_chunks:
            up(j + 1)
        bufg, bufv = bufs[j % 2]
        cols_g = slice(j * TF, (j + 1) * TF)
        cols_v = slice(D_FF + j * TF, D_FF + (j + 1) * TF)
        last = FFN_CONV - 1
        ug = bufg[BF16_ROWS:BF16_ROWS + TM, :] * cw_ref[last:last + 1, cols_g] + cb_ref[:, cols_g]
        uv = bufv[BF16_ROWS:BF16_ROWS + TM, :] * cw_ref[last:last + 1, cols_v] + cb_ref[:, cols_v]
        for kk in range(last):
            ug = ug + bufg[pl.ds(first + kk, TM), :] * cw_ref[kk:kk + 1, cols_g]
            uv = uv + bufv[pl.ds(first + kk, TM), :] * cw_ref[kk:kk + 1, cols_v]
        act_ref[:, cols_g] = (ug * _sigmoid(ug) * uv).astype(BF16)
        if (j + 1) % DOWN_GROUP == 0 or j + 1 == n_chunks:
            lo = (j // DOWN_GROUP) * DOWN_GROUP * TF
            part = _dot(act_ref[:, lo:(j + 1) * TF], wd_ref[lo:(j + 1) * TF, :])
            acc = part if acc is None else acc + part
    x2 = x1 + mod_ref[0, 5:6, :] * acc
    if final_norm:
        x2 = _rms(x2) * gfin_ref[...]
    o_ref[0] = x2


def _mlp(l, yT, yssd, x, mod, g_att, w_out, g_ffn, w_up, conv_w, conv_b, w_down, g_final, final_norm):
    bsz, s, d = x.shape
    resident = dict(pipeline_mode=pl.Buffered(1))
    conv_buf = pltpu.VMEM((BF16_ROWS + TM, TF), F32)

    def weight_spec(shape):
        return pl.BlockSpec((None,) + shape, lambda b, i: (l, 0, 0), **resident)

    return pl.pallas_call(
        functools.partial(_mlp_kernel, final_norm=final_norm),
        grid=(bsz, s // TM),
        in_specs=[pl.BlockSpec((1, D_ATT, TM), lambda b, i: (b, 0, i)),
                  pl.BlockSpec((1, TM, D_SSD), lambda b, i: (b, i, 0)),
                  pl.BlockSpec((1, TM, d), lambda b, i: (b, i, 0)),
                  _mod_spec(l),
                  _layer_spec((1, D_ATT), l),
                  weight_spec((D_ATT + D_SSD, d)),
                  _layer_spec((1, d), l),
                  weight_spec((d, 2 * D_FF)),
                  _layer_spec((FFN_CONV, 2 * D_FF), l),
                  _layer_spec((1, 2 * D_FF), l),
                  weight_spec((D_FF, d)),
                  pl.BlockSpec((1, d), lambda b, i: (0, 0))],
        out_specs=pl.BlockSpec((1, TM, d), lambda b, i: (b, i, 0)),
        out_shape=jax.ShapeDtypeStruct((bsz, s, d), F32),
        scratch_shapes=[pltpu.VMEM((BF16_ROWS, d), BF16), conv_buf, conv_buf, conv_buf, conv_buf,
                        pltpu.VMEM((TM, D_FF), BF16)],
        compiler_params=pltpu.CompilerParams(
            dimension_semantics=("parallel", "arbitrary"), vmem_limit_bytes=VMEM_LIMIT),
        name="mlp",
    )(yT, yssd, x, mod, g_att, w_out, g_ffn, w_up, conv_w, conv_b, w_down, g_final)


def _expand_matrix():
    e = np.zeros((LANES, D_SSD), np.float32)
    for hd in range(N_HEADS):
        e[DT_LO + hd, hd * HEAD_DIM:(hd + 1) * HEAD_DIM] = 1.0
    return jnp.asarray(e, BF16)


def _small_rows(f_vals, dt_vals):
    depth = f_vals.shape[0]
    pad = jnp.zeros((depth, LANES - F_LO - N_HEADS), F32)
    return jnp.concatenate([f_vals, dt_vals, f_vals, f_vals, pad], axis=1)[:, None, :]


def kernel(x, c, mod_w, mod_b, norm_mix_g, norm_ffn_g, w_in, fox_forget_b, attn_norm_g, ssd_conv_w, ssd_conv_b, ssd_dt_bias, ssd_a_log, ssd_d, ssd_norm_g, w_out, ffn_w_up, ffn_conv_w, ffn_conv_b, ffn_w_down, final_g):
    depth = w_in.shape[0]
    bsz = x.shape[0]
    mod = _modulation(c, mod_w, mod_b).reshape(depth, bsz, N_MOD, D_MODEL)
    expand = _expand_matrix()

    o_k, o_v, o_f = D_ATT, 2 * D_ATT, 3 * D_ATT
    o_z = o_f + N_HEADS
    o_x = o_z + D_SSD
    o_dt = o_x + D_XBC
    w_f = w_in[:, :, o_f:o_z]
    w_dt = w_in[:, :, o_dt:o_dt + N_HEADS]
    pad = jnp.zeros((depth, D_MODEL, LANES - F_LO - N_HEADS), F32)
    w_rows = jnp.concatenate([w_in[:, :, :o_k], w_in[:, :, o_v:o_f]], axis=2).transpose(0, 2, 1).astype(BF16)
    w_cols = jnp.concatenate([w_in[:, :, o_k:o_v], w_in[:, :, o_z:o_x], w_in[:, :, o_x:o_dt],
                              w_f, w_dt, w_f, w_f, pad], axis=2).astype(BF16)
    bias_small = _small_rows(fox_forget_b, ssd_dt_bias)
    alog_rows = _small_rows(jnp.zeros_like(ssd_a_log), ssd_a_log)
    dskip_rows = jnp.repeat(ssd_d, HEAD_DIM, axis=1)[:, None, :]
    row = lambda a: a[:, None, :]
    w_out_b = w_out.astype(BF16)
    w_up_b = ffn_w_up.astype(BF16)
    w_down_b = ffn_w_down.astype(BF16)
    g_final = final_g.reshape(1, -1)

    for l in range(depth):
        qT, vT, k, fparts, yssd = _mixer_in(
            l, x, mod, row(norm_mix_g), w_rows, w_cols, bias_small, ssd_conv_w, row(ssd_conv_b),
            alog_rows, dskip_rows, row(ssd_norm_g), expand)
        yT = _attention(qT, k, fparts, vT)
        x = _mlp(l, yT, yssd, x, mod, row(attn_norm_g), w_out_b, row(norm_ffn_g), w_up_b, ffn_conv_w,
                 row(ffn_conv_b), w_down_b, g_final, final_norm=(l == depth - 1))
    return x
```

```python
import functools

import numpy as np
import jax
import jax.numpy as jnp
from jax import lax
from jax.experimental import pallas as pl
from jax.experimental.pallas import tpu as pltpu

F32 = jnp.float32
BF16 = jnp.bfloat16

D_MODEL = 1024
N_HEADS = 8
HEAD_DIM = 64
D_ATT = 512
D_SSD = 512
SSD_GROUPS = 2
SSD_STATE = 128
SSD_CONV = 4
SSD_CHUNK = 128
D_XBC = D_SSD + 2 * SSD_GROUPS * SSD_STATE
D_FF = 2816
FFN_CONV = 3
N_MOD = 6
EPS = 1e-6
NEG = -1e30
LOG2E = 1.4426950408889634

LANES = 128
BF16_ROWS = 16
VMEM_LIMIT = 56 * 1024 * 1024

TM = 512
TQ = 512
TK = 512
TF = 256
DOWN_GROUP = 6

F_HI, DT_LO, F_MID, F_LO = 0, 8, 16, 24


def _sigmoid(x):
    return 1.0 / (1.0 + jnp.exp(-x))


def _split2(x):
    hi = x.astype(BF16)
    lo = (x - hi.astype(F32)).astype(BF16)
    return hi, lo


def _split3(x):
    hi = x.astype(BF16)
    r = x - hi.astype(F32)
    mid = r.astype(BF16)
    lo = (r - mid.astype(F32)).astype(BF16)
    return hi, mid, lo


def _dot(a, b):
    return jnp.dot(a, b, preferred_element_type=F32)


def _dot_nt(a, b):
    return lax.dot_general(a, b, (((1,), (1,)), ((), ())), preferred_element_type=F32)


def _tril(n):
    r = lax.broadcasted_iota(jnp.int32, (n, n), 0)
    c = lax.broadcasted_iota(jnp.int32, (n, n), 1)
    return jnp.where(c <= r, 1.0, 0.0).astype(BF16)


def _rms(x):
    return x * lax.rsqrt(jnp.mean(x * x, axis=-1, keepdims=True) + EPS)


def _mod_kernel(c_ref, w_ref, b_ref, o_ref):
    c = c_ref[...]
    o_ref[0] = _dot(c * _sigmoid(c), w_ref[0]) + b_ref[0]


def _modulation(c, mod_w, mod_b):
    depth, d, n = mod_w.shape
    bsz = c.shape[0]
    tn = 1536
    return pl.pallas_call(
        _mod_kernel,
        grid=(depth, n // tn),
        in_specs=[pl.BlockSpec((bsz, d), lambda l, j: (0, 0)),
                  pl.BlockSpec((1, d, tn), lambda l, j: (l, 0, j)),
                  pl.BlockSpec((1, 1, tn), lambda l, j: (l, 0, j))],
        out_specs=pl.BlockSpec((1, bsz, tn), lambda l, j: (l, 0, j)),
        out_shape=jax.ShapeDtypeStruct((depth, bsz, n), F32),
        compiler_params=pltpu.CompilerParams(
            dimension_semantics=("parallel", "parallel"), vmem_limit_bytes=VMEM_LIMIT),
        name="modulation",
    )(c, mod_w, mod_b.reshape(depth, 1, n))


def _mixer_in_kernel(x_ref, mod_ref, g_ref, wc_ref, bs_ref, cw_ref, cb_ref, alog_ref, dskip_ref,
                     gn_ref, e_ref, qT_ref, vT_ref, k_ref, fp_ref, y_ref, carry_ref, buf_ref, state_ref):
    @pl.when(pl.program_id(1) == 0)
    def _():
        carry_ref[...] = jnp.zeros_like(carry_ref)
        state_ref[...] = jnp.zeros_like(state_ref)
        buf_ref[0:BF16_ROWS, :] = jnp.zeros((BF16_ROWS, D_XBC), F32)

    x = x_ref[0]
    y = _rms(x) * g_ref[...]
    h = (y * (1.0 + mod_ref[0, 1:2, :]) + mod_ref[0, 0:1, :]).astype(BF16)

    o_v, o_k, o_z, o_x = D_ATT, 2 * D_ATT, 3 * D_ATT, 3 * D_ATT + D_SSD
    buf_ref[BF16_ROWS:BF16_ROWS + TM, :] = _dot(h, wc_ref[:, o_x:o_x + D_XBC])
    small = _dot(h, wc_ref[:, o_x + D_XBC:o_x + D_XBC + LANES]) + bs_ref[...]

    z = _dot(h, wc_ref[:, o_z:o_z + D_SSD])

    conv = buf_ref[BF16_ROWS:BF16_ROWS + TM, :] * cw_ref[SSD_CONV - 1:SSD_CONV, :] + cb_ref[...]
    for kk in range(SSD_CONV - 1):
        conv = conv + buf_ref[pl.ds(BF16_ROWS - SSD_CONV + 1 + kk, TM), :] * cw_ref[kk:kk + 1, :]
    buf_ref[0:BF16_ROWS, :] = buf_ref[TM:TM + BF16_ROWS, :]
    xc = conv * _sigmoid(conv)
    xs = xc[:, :D_SSD]
    gn = SSD_GROUPS * SSD_STATE
    bm = xc[:, D_SSD:D_SSD + gn]
    cm = xc[:, D_SSD + gn:]

    t = jnp.log1p(jnp.exp(-jnp.abs(small)))
    dt_all = jnp.maximum(small, 0.0) + t
    nlf = (jnp.maximum(-small, 0.0) + t) * LOG2E

    lane = lax.broadcasted_iota(jnp.int32, (1, LANES), 1) // 8
    head_lanes = lane == DT_LO // 8
    a_neg = jnp.where(head_lanes, -jnp.exp(alog_ref[...]), 0.0)
    dt_small = jnp.where(head_lanes, dt_all, 0.0)
    a_small = dt_small * a_neg
    e = e_ref[...]
    dhi, dlo = _split2(dt_small)
    dt_exp = _dot(dhi, e) + _dot(dlo, e)

    tri = _tril(SSD_CHUNK)
    rr = lax.broadcasted_iota(jnp.int32, (SSD_CHUNK, SSD_CHUNK), 0)
    cc = lax.broadcasted_iota(jnp.int32, (SSD_CHUNK, SSD_CHUNK), 1)
    causal = cc <= rr
    lane2 = lax.broadcasted_iota(jnp.int32, (SSD_CHUNK, LANES), 1)
    heads_per_group = N_HEADS // SSD_GROUPS
    gw = heads_per_group * HEAD_DIM

    def chunk_pre(c):
        rows_c = slice(c * SSD_CHUNK, (c + 1) * SSD_CHUNK)
        ahi, alo = _split2(a_small[rows_c])
        cs = _dot(tri, ahi) + _dot(tri, alo)
        chi, clo = _split2(cs)
        cs_exp = _dot(chi, e) + _dot(clo, e)
        cs_t = cs.T
        cbs, y_offs = [], []
        for g in range(SSD_GROUPS):
            bg = bm[rows_c, g * SSD_STATE:(g + 1) * SSD_STATE]
            cg = cm[rows_c, g * SSD_STATE:(g + 1) * SSD_STATE].astype(BF16)
            cbs.append(_dot_nt(cg, bg.astype(BF16)))
            y_offs.append(_dot(cg, state_ref[:, g * gw:(g + 1) * gw].astype(BF16)))
        return cs, cs_exp, cs_t, cbs, y_offs

    def chunk_main(c, pre):
        cs, cs_exp, cs_t, cbs, y_offs = pre
        rows_c = slice(c * SSD_CHUNK, (c + 1) * SSD_CHUNK)
        last = cs_exp[SSD_CHUNK - 1:SSD_CHUNK, :]
        dec_end = jnp.exp(last - cs_exp)
        dec_start = jnp.exp(cs_exp)
        dec_chunk = jnp.exp(last)
        xs_c = xs[rows_c]
        xdt = xs_c * dt_exp[rows_c]
        xde = (xdt * dec_end).astype(BF16)
        outs = []
        for g in range(SSD_GROUPS):
            bg = bm[rows_c, g * SSD_STATE:(g + 1) * SSD_STATE]
            cb = cbs[g]
            y_off = y_offs[g] * dec_start[:, g * gw:(g + 1) * gw]
            pair_out = []
            for pr in range(heads_per_group // 2):
                lo_l = g * gw + pr * LANES
                xp = xdt[:, lo_l:lo_l + LANES]
                acc = None
                for half in range(2):
                    hd = g * heads_per_group + pr * 2 + half
                    col = cs[:, DT_LO + hd:DT_LO + hd + 1]
                    row = cs_t[DT_LO + hd:DT_LO + hd + 1, :]
                    seg = jnp.where(causal, col - row, NEG)
                    mmat = (cb * jnp.exp(seg)).astype(BF16)
                    keep_half = (lane2 < HEAD_DIM) if half == 0 else (lane2 >= HEAD_DIM)
                    xh = jnp.where(keep_half, xp, 0.0).astype(BF16)
                    term = _dot(mmat, xh)
                    acc = term if acc is None else acc + term
                pair_out.append(acc)
            y_diag = jnp.concatenate(pair_out, axis=1)
            outs.append(y_diag + y_off)
            st = state_ref[:, g * gw:(g + 1) * gw]
            new_state = st * dec_chunk[:, g * gw:(g + 1) * gw] + _dot(bg.T.astype(BF16), xde[:, g * gw:(g + 1) * gw])
            state_ref[:, g * gw:(g + 1) * gw] = new_state
        yc = jnp.concatenate(outs, axis=1) + xs_c * dskip_ref[...]
        zc = z[rows_c]
        yc = yc * (zc * _sigmoid(zc))
        normed = [_rms(yc[:, g * gw:(g + 1) * gw]) for g in range(SSD_GROUPS)]
        y_ref[0, rows_c, :] = (jnp.concatenate(normed, axis=1) * gn_ref[...]).astype(BF16)

    pre = chunk_pre(0)
    qT_ref[0] = (_dot(h, wc_ref[:, 0:D_ATT]) * (LOG2E * HEAD_DIM ** -0.5)).T.astype(BF16)
    chunk_main(0, pre)
    pre = chunk_pre(1)
    vT_ref[0, 0] = _dot(h, wc_ref[:, o_v:o_v + D_ATT]).T.astype(BF16)
    chunk_main(1, pre)
    pre = chunk_pre(2)
    k_ref[0] = _dot(h, wc_ref[:, o_k:o_k + D_ATT]).astype(BF16)
    chunk_main(2, pre)
    pre = chunk_pre(3)

    tril = _tril(TM)
    hi, mid, lo = _split3(nlf)
    cum = _dot(tril, hi) + _dot(tril, mid) + _dot(tril, lo) + carry_ref[...]
    carry_ref[...] = cum[TM - 1:TM, :]
    chunk_main(3, pre)
    hi, mid, lo = _split3(cum)
    lane_f = lax.broadcasted_iota(jnp.int32, cum.shape, 1) // 8
    parts = jnp.where(lane_f == F_HI // 8, hi.astype(F32),
                      jnp.where(lane_f == F_MID // 8, mid.astype(F32),
                                jnp.where(lane_f == F_LO // 8, lo.astype(F32), 0.0)))
    fp_ref[0] = parts.astype(BF16)


def _layer_spec(shape, l):
    zeros = (0,) * len(shape)
    return pl.BlockSpec((None,) + tuple(shape), lambda b, i: (l,) + zeros)


def _mod_spec(l):
    return pl.BlockSpec((None, 1, N_MOD, D_MODEL), lambda b, i: (l, b, 0, 0))


def _mixer_in(l, x, mod, g, w_cols, bias_small, conv_w, conv_b, alog_row, dskip_row, gnorm_row, expand):
    bsz, s, d = x.shape
    ns = s // TM
    ncol = w_cols.shape[-1]
    out_shape = (
        jax.ShapeDtypeStruct((bsz, D_ATT, s), BF16),
        jax.ShapeDtypeStruct((bsz, ns, D_ATT, TM), BF16),
        jax.ShapeDtypeStruct((bsz, s, D_ATT), BF16),
        jax.ShapeDtypeStruct((bsz, s, LANES), BF16),
        jax.ShapeDtypeStruct((bsz, s, D_SSD), BF16),
    )
    return pl.pallas_call(
        _mixer_in_kernel,
        grid=(bsz, ns),
        in_specs=[pl.BlockSpec((1, TM, d), lambda b, i: (b, i, 0)),
                  _mod_spec(l),
                  _layer_spec((1, d), l),
                  _layer_spec((d, ncol), l),
                  _layer_spec((1, LANES), l),
                  _layer_spec((SSD_CONV, D_XBC), l),
                  _layer_spec((1, D_XBC), l),
                  _layer_spec((1, LANES), l),
                  _layer_spec((1, D_SSD), l),
                  _layer_spec((1, D_SSD), l),
                  pl.BlockSpec((LANES, D_SSD), lambda b, i: (0, 0))],
        out_specs=(pl.BlockSpec((1, D_ATT, TM), lambda b, i: (b, 0, i)),
                   pl.BlockSpec((1, 1, D_ATT, TM), lambda b, i: (b, i, 0, 0)),
                   pl.BlockSpec((1, TM, D_ATT), lambda b, i: (b, i, 0)),
                   pl.BlockSpec((1, TM, LANES), lambda b, i: (b, i, 0)),
                   pl.BlockSpec((1, TM, D_SSD), lambda b, i: (b, i, 0))),
        out_shape=out_shape,
        scratch_shapes=[pltpu.VMEM((1, LANES), F32),
                        pltpu.VMEM((BF16_ROWS + TM, D_XBC), F32),
                        pltpu.VMEM((SSD_STATE, D_SSD), F32)],
        compiler_params=pltpu.CompilerParams(
            dimension_semantics=("parallel", "arbitrary"), vmem_limit_bytes=VMEM_LIMIT),
        name="mixer_in",
    )(x, mod, g, w_cols, bias_small, conv_w, conv_b, alog_row, dskip_row, gnorm_row, expand)


def _attn_kernel(qT_ref, k_ref, f_ref, vT_ref, o_ref, s0_ref, s1_ref, acc_ref):
    h = pl.program_id(1)
    odd = h % 2
    nq = qT_ref.shape[2] // TQ
    r = lax.broadcasted_iota(jnp.int32, (LANES, TQ), 0)
    sel = jnp.where(r == h + F_HI, 1.0, jnp.where(r == h + F_MID, 1.0, jnp.where(r == h + F_LO, 1.0, 0.0)))
    sel = sel.astype(BF16)
    ones_rows = jnp.ones((BF16_ROWS, TK), BF16)
    bufs = (s0_ref, s1_ref)

    def make_qa(i):
        q = qT_ref[0, :, i * TQ:(i + 1) * TQ]
        zq = jnp.zeros_like(q)
        return jnp.concatenate([jnp.where(odd == 0, q, zq), jnp.where(odd == 1, q, zq), sel], axis=0)

    hk = TK // 2

    def logits(qa, i, t, dst):
        ka = jnp.concatenate([k_ref[0, t * TK:(t + 1) * TK, :], f_ref[0, t * TK:(t + 1) * TK, :]], axis=1)
        if t < i:
            dst[...] = _dot(ka, qa)
        else:
            dst[0:hk, :] = _dot(ka[0:hk], qa)
            dst[hk:, hk:] = _dot(ka[hk:], qa[:, hk:])

    def below_diagonal(rows, cols):
        key = lax.broadcasted_iota(jnp.int32, (rows, cols), 0)
        qry = lax.broadcasted_iota(jnp.int32, (rows, cols), 1)
        return key <= qry

    steps = [(i, t) for i in range(nq) for t in range(i + 1)]
    qa = make_qa(0)
    logits(qa, 0, 0, bufs[0])
    m = None
    for n, (i, t) in enumerate(steps):
        if n + 1 < len(steps):
            i2, t2 = steps[n + 1]
            qa_next = qa if i2 == i else make_qa(i2)
            logits(qa_next, i2, t2, bufs[(n + 1) % 2])
        va = jnp.concatenate([vT_ref[0, t], ones_rows], axis=0)
        if t < i:
            st = bufs[n % 2][...]
            mx = jnp.max(st, axis=0, keepdims=True)
            m_new = mx if t == 0 else jnp.maximum(m, mx)
            pv = _dot(va, jnp.exp2((st - m_new).astype(BF16)))
        else:
            top = jnp.where(below_diagonal(hk, TQ), bufs[n % 2][0:hk, :], NEG)
            bot = jnp.where(below_diagonal(hk, hk), bufs[n % 2][hk:, hk:], NEG)
            mx = jnp.max(top, axis=0, keepdims=True)
            mx = jnp.concatenate(
                [mx[:, :hk], jnp.maximum(mx[:, hk:], jnp.max(bot, axis=0, keepdims=True))], axis=1)
            m_new = mx if t == 0 else jnp.maximum(m, mx)
            pv = _dot(va[:, 0:hk], jnp.exp2((top - m_new).astype(BF16)))
            pv_bot = _dot(va[:, hk:], jnp.exp2((bot - m_new[:, hk:]).astype(BF16)))
            pv = jnp.concatenate([pv[:, :hk], pv[:, hk:] + pv_bot], axis=1)
        acc = pv if t == 0 else jnp.exp2(m - m_new) * acc_ref[...] + pv
        if t == i:
            o_ref[0, :, i * TQ:(i + 1) * TQ] = acc[:HEAD_DIM] * (1.0 / acc[HEAD_DIM:HEAD_DIM + 1])
        else:
            acc_ref[...] = acc
        m = m_new
        if n + 1 < len(steps):
            qa = qa_next


def _attention(qT, k, fparts, vT):
    bsz, _, s = qT.shape
    nk = s // TK
    return pl.pallas_call(
        _attn_kernel,
        grid=(bsz, N_HEADS),
        in_specs=[pl.BlockSpec((1, HEAD_DIM, s), lambda b, h: (b, h, 0)),
                  pl.BlockSpec((1, s, LANES), lambda b, h: (b, 0, h // 2)),
                  pl.BlockSpec((1, s, LANES), lambda b, h: (b, 0, 0)),
                  pl.BlockSpec((1, nk, HEAD_DIM, TK), lambda b, h: (b, 0, h, 0))],
        out_specs=pl.BlockSpec((1, HEAD_DIM, s), lambda b, h: (b, h, 0)),
        out_shape=jax.ShapeDtypeStruct((bsz, D_ATT, s), F32),
        scratch_shapes=[pltpu.VMEM((TK, TQ), F32), pltpu.VMEM((TK, TQ), F32),
                        pltpu.VMEM((HEAD_DIM + BF16_ROWS, TQ), F32)],
        compiler_params=pltpu.CompilerParams(
            dimension_semantics=("parallel", "parallel"), vmem_limit_bytes=VMEM_LIMIT),
        name="attention",
    )(qT, k, fparts, vT)


def _mlp_kernel(yT_ref, ys_ref, x_ref, mod_ref, ga_ref, wo_ref, gf_ref, wu_ref, cw_ref, cb_ref, wd_ref,
                gfin_ref, o_ref, halo_ref, ga0_ref, va0_ref, ga1_ref, va1_ref, act_ref, *, final_norm):
    ya = _rms(yT_ref[0].T) * ga_ref[...]
    ycat = jnp.concatenate([ya.astype(BF16), ys_ref[0]], axis=1)
    x1 = x_ref[0] + mod_ref[0, 2:3, :] * _dot(ycat, wo_ref[...])
    h2 = _rms(x1) * gf_ref[...]
    h2 = (h2 * (1.0 + mod_ref[0, 4:5, :]) + mod_ref[0, 3:4, :]).astype(BF16)

    @pl.when(pl.program_id(1) == 0)
    def _():
        halo_ref[...] = jnp.zeros_like(halo_ref)

    hext = jnp.concatenate([halo_ref[...], h2], axis=0)
    halo_ref[...] = h2[TM - BF16_ROWS:, :]

    first = BF16_ROWS - FFN_CONV + 1
    bufs = ((ga0_ref, va0_ref), (ga1_ref, va1_ref))
    n_chunks = D_FF // TF

    def up(j):
        bufg, bufv = bufs[j % 2]
        bufg[...] = _dot(hext, wu_ref[:, j * TF:(j + 1) * TF])
        bufv[...] = _dot(hext, wu_ref[:, D_FF + j * TF:D_FF + (j + 1) * TF])

    up(0)
    acc = None
    for j in range(n_chunks):
        if j + 1 < n_chunks:
            up(j + 1)
        bufg, bufv = bufs[j % 2]
        cols_g = slice(j * TF, (j + 1) * TF)
        cols_v = slice(D_FF + j * TF, D_FF + (j + 1) * TF)
        last = FFN_CONV - 1
        ug = bufg[BF16_ROWS:BF16_ROWS + TM, :] * cw_ref[last:last + 1, cols_g] + cb_ref[:, cols_g]
        uv = bufv[BF16_ROWS:BF16_ROWS + TM, :] * cw_ref[last:last + 1, cols_v] + cb_ref[:, cols_v]
        for kk in range(last):
            ug = ug + bufg[pl.ds(first + kk, TM), :] * cw_ref[kk:kk + 1, cols_g]
            uv = uv + bufv[pl.ds(first + kk, TM), :] * cw_ref[kk:kk + 1, cols_v]
        act_ref[:, cols_g] = (ug * _sigmoid(ug) * uv).astype(BF16)
        if (j + 1) % DOWN_GROUP == 0 or j + 1 == n_chunks:
            lo = (j // DOWN_GROUP) * DOWN_GROUP * TF
            part = _dot(act_ref[:, lo:(j + 1) * TF], wd_ref[lo:(j + 1) * TF, :])
            acc = part if acc is None else acc + part
    x2 = x1 + mod_ref[0, 5:6, :] * acc
    if final_norm:
        x2 = _rms(x2) * gfin_ref[...]
    o_ref[0] = x2


def _mlp(l, yT, yssd, x, mod, g_att, w_out, g_ffn, w_up, conv_w, conv_b, w_down, g_final, final_norm):
    bsz, s, d = x.shape
    resident = dict(pipeline_mode=pl.Buffered(1))
    conv_buf = pltpu.VMEM((BF16_ROWS + TM, TF), F32)

    def weight_spec(shape):
        return pl.BlockSpec((None,) + shape, lambda b, i: (l, 0, 0), **resident)

    return pl.pallas_call(
        functools.partial(_mlp_kernel, final_norm=final_norm),
        grid=(bsz, s // TM),
        in_specs=[pl.BlockSpec((1, D_ATT, TM), lambda b, i: (b, 0, i)),
                  pl.BlockSpec((1, TM, D_SSD), lambda b, i: (b, i, 0)),
                  pl.BlockSpec((1, TM, d), lambda b, i: (b, i, 0)),
                  _mod_spec(l),
                  _layer_spec((1, D_ATT), l),
                  weight_spec((D_ATT + D_SSD, d)),
                  _layer_spec((1, d), l),
                  weight_spec((d, 2 * D_FF)),
                  _layer_spec((FFN_CONV, 2 * D_FF), l),
                  _layer_spec((1, 2 * D_FF), l),
                  weight_spec((D_FF, d)),
                  pl.BlockSpec((1, d), lambda b, i: (0, 0))],
        out_specs=pl.BlockSpec((1, TM, d), lambda b, i: (b, i, 0)),
        out_shape=jax.ShapeDtypeStruct((bsz, s, d), F32),
        scratch_shapes=[pltpu.VMEM((BF16_ROWS, d), BF16), conv_buf, conv_buf, conv_buf, conv_buf,
                        pltpu.VMEM((TM, D_FF), BF16)],
        compiler_params=pltpu.CompilerParams(
            dimension_semantics=("parallel", "arbitrary"), vmem_limit_bytes=VMEM_LIMIT),
        name="mlp",
    )(yT, yssd, x, mod, g_att, w_out, g_ffn, w_up, conv_w, conv_b, w_down, g_final)


def _expand_matrix():
    e = np.zeros((LANES, D_SSD), np.float32)
    for hd in range(N_HEADS):
        e[DT_LO + hd, hd * HEAD_DIM:(hd + 1) * HEAD_DIM] = 1.0
    return jnp.asarray(e, BF16)


def _small_rows(f_vals, dt_vals):
    depth = f_vals.shape[0]
    pad = jnp.zeros((depth, LANES - F_LO - N_HEADS), F32)
    return jnp.concatenate([f_vals, dt_vals, f_vals, f_vals, pad], axis=1)[:, None, :]


def kernel(x, c, mod_w, mod_b, norm_mix_g, norm_ffn_g, w_in, fox_forget_b, attn_norm_g, ssd_conv_w, ssd_conv_b, ssd_dt_bias, ssd_a_log, ssd_d, ssd_norm_g, w_out, ffn_w_up, ffn_conv_w, ffn_conv_b, ffn_w_down, final_g):
    depth = w_in.shape[0]
    bsz = x.shape[0]
    mod = _modulation(c, mod_w, mod_b).reshape(depth, bsz, N_MOD, D_MODEL)
    expand = _expand_matrix()

    o_k, o_v, o_f = D_ATT, 2 * D_ATT, 3 * D_ATT
    o_z = o_f + N_HEADS
    o_x = o_z + D_SSD
    o_dt = o_x + D_XBC
    w_f = w_in[:, :, o_f:o_z]
    w_dt = w_in[:, :, o_dt:o_dt + N_HEADS]
    pad = jnp.zeros((depth, D_MODEL, LANES - F_LO - N_HEADS), F32)
    w_cols = jnp.concatenate([w_in[:, :, :o_k], w_in[:, :, o_v:o_f], w_in[:, :, o_k:o_v], w_in[:, :, o_z:o_x],
                              w_in[:, :, o_x:o_dt], w_f, w_dt, w_f, w_f, pad], axis=2).astype(BF16)
    bias_small = _small_rows(fox_forget_b, ssd_dt_bias)
    alog_rows = _small_rows(jnp.zeros_like(ssd_a_log), ssd_a_log)
    dskip_rows = jnp.repeat(ssd_d, HEAD_DIM, axis=1)[:, None, :]
    row = lambda a: a[:, None, :]
    w_out_b = w_out.astype(BF16)
    w_up_b = ffn_w_up.astype(BF16)
    w_down_b = ffn_w_down.astype(BF16)
    g_final = final_g.reshape(1, -1)

    for l in range(depth):
        qT, vT, k, fparts, yssd = _mixer_in(
            l, x, mod, row(norm_mix_g), w_cols, bias_small, ssd_conv_w, row(ssd_conv_b),
            alog_rows, dskip_rows, row(ssd_norm_g), expand)
        yT = _attention(qT, k, fparts, vT)
        x = _mlp(l, yT, yssd, x, mod, row(attn_norm_g), w_out_b, row(norm_ffn_g), w_up_b, ffn_conv_w,
                 row(ffn_conv_b), w_down_b, g_final, final_norm=(l == depth - 1))
    return x
```

```python
import functools

import numpy as np
import jax
import jax.numpy as jnp
from jax import lax
from jax.experimental import pallas as pl
from jax.experimental.pallas import tpu as pltpu

F32 = jnp.float32
BF16 = jnp.bfloat16

D_MODEL = 1024
N_HEADS = 8
HEAD_DIM = 64
D_ATT = 512
D_SSD = 512
SSD_GROUPS = 2
SSD_STATE = 128
SSD_CONV = 4
SSD_CHUNK = 128
D_XBC = D_SSD + 2 * SSD_GROUPS * SSD_STATE
D_FF = 2816
FFN_CONV = 3
N_MOD = 6
EPS = 1e-6
NEG = -1e30
LOG2E = 1.4426950408889634

LANES = 128
BF16_ROWS = 16
VMEM_LIMIT = 56 * 1024 * 1024

TM = 512
TQ = 512
TK = 512
TF = 256
DOWN_GROUP = 6

F_HI, DT_LO, F_MID, F_LO = 0, 8, 16, 24


def _sigmoid(x):
    return 1.0 / (1.0 + jnp.exp(-x))


def _split2(x):
    hi = x.astype(BF16)
    lo = (x - hi.astype(F32)).astype(BF16)
    return hi, lo


def _split3(x):
    hi = x.astype(BF16)
    r = x - hi.astype(F32)
    mid = r.astype(BF16)
    lo = (r - mid.astype(F32)).astype(BF16)
    return hi, mid, lo


def _dot(a, b):
    return jnp.dot(a, b, preferred_element_type=F32)


def _dot_nt(a, b):
    return lax.dot_general(a, b, (((1,), (1,)), ((), ())), preferred_element_type=F32)


def _tril(n):
    r = lax.broadcasted_iota(jnp.int32, (n, n), 0)
    c = lax.broadcasted_iota(jnp.int32, (n, n), 1)
    return jnp.where(c <= r, 1.0, 0.0).astype(BF16)


def _rms(x):
    return x * lax.rsqrt(jnp.mean(x * x, axis=-1, keepdims=True) + EPS)


def _mod_kernel(c_ref, w_ref, b_ref, o_ref):
    c = c_ref[...]
    o_ref[0] = _dot(c * _sigmoid(c), w_ref[0]) + b_ref[0]


def _modulation(c, mod_w, mod_b):
    depth, d, n = mod_w.shape
    bsz = c.shape[0]
    tn = 1536
    return pl.pallas_call(
        _mod_kernel,
        grid=(depth, n // tn),
        in_specs=[pl.BlockSpec((bsz, d), lambda l, j: (0, 0)),
                  pl.BlockSpec((1, d, tn), lambda l, j: (l, 0, j)),
                  pl.BlockSpec((1, 1, tn), lambda l, j: (l, 0, j))],
        out_specs=pl.BlockSpec((1, bsz, tn), lambda l, j: (l, 0, j)),
        out_shape=jax.ShapeDtypeStruct((depth, bsz, n), F32),
        compiler_params=pltpu.CompilerParams(
            dimension_semantics=("parallel", "parallel"), vmem_limit_bytes=VMEM_LIMIT),
        name="modulation",
    )(c, mod_w, mod_b.reshape(depth, 1, n))


def _mixer_in_kernel(x_ref, mod_ref, g_ref, wr_ref, wc_ref, bs_ref, cw_ref, cb_ref, alog_ref, dskip_ref,
                     gn_ref, e_ref, qT_ref, vT_ref, k_ref, fp_ref, y_ref, carry_ref, buf_ref, state_ref):
    @pl.when(pl.program_id(1) == 0)
    def _():
        carry_ref[...] = jnp.zeros_like(carry_ref)
        state_ref[...] = jnp.zeros_like(state_ref)
        buf_ref[0:BF16_ROWS, :] = jnp.zeros((BF16_ROWS, D_XBC), F32)

    x = x_ref[0]
    y = _rms(x) * g_ref[...]
    h = (y * (1.0 + mod_ref[0, 1:2, :]) + mod_ref[0, 0:1, :]).astype(BF16)

    o = D_ATT + D_SSD
    buf_ref[BF16_ROWS:BF16_ROWS + TM, :] = _dot(h, wc_ref[:, o:o + D_XBC])
    small = _dot(h, wc_ref[:, o + D_XBC:o + D_XBC + LANES]) + bs_ref[...]

    z = _dot(h, wc_ref[:, D_ATT:D_ATT + D_SSD])

    conv = buf_ref[BF16_ROWS:BF16_ROWS + TM, :] * cw_ref[SSD_CONV - 1:SSD_CONV, :] + cb_ref[...]
    for kk in range(SSD_CONV - 1):
        conv = conv + buf_ref[pl.ds(BF16_ROWS - SSD_CONV + 1 + kk, TM), :] * cw_ref[kk:kk + 1, :]
    buf_ref[0:BF16_ROWS, :] = buf_ref[TM:TM + BF16_ROWS, :]
    xc = conv * _sigmoid(conv)
    xs = xc[:, :D_SSD]
    gn = SSD_GROUPS * SSD_STATE
    bm = xc[:, D_SSD:D_SSD + gn]
    cm = xc[:, D_SSD + gn:]

    t = jnp.log1p(jnp.exp(-jnp.abs(small)))
    dt_all = jnp.maximum(small, 0.0) + t
    nlf = (jnp.maximum(-small, 0.0) + t) * LOG2E

    lane = lax.broadcasted_iota(jnp.int32, (1, LANES), 1) // 8
    head_lanes = lane == DT_LO // 8
    a_neg = jnp.where(head_lanes, -jnp.exp(alog_ref[...]), 0.0)
    dt_small = jnp.where(head_lanes, dt_all, 0.0)
    a_small = dt_small * a_neg
    e = e_ref[...]
    dhi, dlo = _split2(dt_small)
    dt_exp = _dot(dhi, e) + _dot(dlo, e)

    tri = _tril(SSD_CHUNK)
    rr = lax.broadcasted_iota(jnp.int32, (SSD_CHUNK, SSD_CHUNK), 0)
    cc = lax.broadcasted_iota(jnp.int32, (SSD_CHUNK, SSD_CHUNK), 1)
    causal = cc <= rr
    lane2 = lax.broadcasted_iota(jnp.int32, (SSD_CHUNK, LANES), 1)
    heads_per_group = N_HEADS // SSD_GROUPS
    gw = heads_per_group * HEAD_DIM

    def chunk_pre(c):
        rows_c = slice(c * SSD_CHUNK, (c + 1) * SSD_CHUNK)
        ahi, alo = _split2(a_small[rows_c])
        cs = _dot(tri, ahi) + _dot(tri, alo)
        chi, clo = _split2(cs)
        cs_exp = _dot(chi, e) + _dot(clo, e)
        cs_t = cs.T
        cbs, y_offs = [], []
        for g in range(SSD_GROUPS):
            bg = bm[rows_c, g * SSD_STATE:(g + 1) * SSD_STATE]
            cg = cm[rows_c, g * SSD_STATE:(g + 1) * SSD_STATE].astype(BF16)
            cbs.append(_dot_nt(cg, bg.astype(BF16)))
            y_offs.append(_dot(cg, state_ref[:, g * gw:(g + 1) * gw].astype(BF16)))
        return cs, cs_exp, cs_t, cbs, y_offs

    def chunk_main(c, pre):
        cs, cs_exp, cs_t, cbs, y_offs = pre
        rows_c = slice(c * SSD_CHUNK, (c + 1) * SSD_CHUNK)
        last = cs_exp[SSD_CHUNK - 1:SSD_CHUNK, :]
        dec_end = jnp.exp(last - cs_exp)
        dec_start = jnp.exp(cs_exp)
        dec_chunk = jnp.exp(last)
        xs_c = xs[rows_c]
        xdt = xs_c * dt_exp[rows_c]
        xde = (xdt * dec_end).astype(BF16)
        outs = []
        for g in range(SSD_GROUPS):
            bg = bm[rows_c, g * SSD_STATE:(g + 1) * SSD_STATE]
            cb = cbs[g]
            y_off = y_offs[g] * dec_start[:, g * gw:(g + 1) * gw]
            pair_out = []
            for pr in range(heads_per_group // 2):
                lo_l = g * gw + pr * LANES
                xp = xdt[:, lo_l:lo_l + LANES]
                acc = None
                for half in range(2):
                    hd = g * heads_per_group + pr * 2 + half
                    col = cs[:, DT_LO + hd:DT_LO + hd + 1]
                    row = cs_t[DT_LO + hd:DT_LO + hd + 1, :]
                    seg = jnp.where(causal, col - row, NEG)
                    mmat = (cb * jnp.exp(seg)).astype(BF16)
                    keep_half = (lane2 < HEAD_DIM) if half == 0 else (lane2 >= HEAD_DIM)
                    xh = jnp.where(keep_half, xp, 0.0).astype(BF16)
                    term = _dot(mmat, xh)
                    acc = term if acc is None else acc + term
                pair_out.append(acc)
            y_diag = jnp.concatenate(pair_out, axis=1)
            outs.append(y_diag + y_off)
            st = state_ref[:, g * gw:(g + 1) * gw]
            new_state = st * dec_chunk[:, g * gw:(g + 1) * gw] + _dot(bg.T.astype(BF16), xde[:, g * gw:(g + 1) * gw])
            state_ref[:, g * gw:(g + 1) * gw] = new_state
        yc = jnp.concatenate(outs, axis=1) + xs_c * dskip_ref[...]
        zc = z[rows_c]
        yc = yc * (zc * _sigmoid(zc))
        normed = [_rms(yc[:, g * gw:(g + 1) * gw]) for g in range(SSD_GROUPS)]
        y_ref[0, rows_c, :] = (jnp.concatenate(normed, axis=1) * gn_ref[...]).astype(BF16)

    pre = chunk_pre(0)
    qT_ref[0] = (_dot_nt(wr_ref[0:D_ATT, :], h) * (LOG2E * HEAD_DIM ** -0.5)).astype(BF16)
    chunk_main(0, pre)
    pre = chunk_pre(1)
    vT_ref[0, 0] = _dot_nt(wr_ref[D_ATT:2 * D_ATT, :], h).astype(BF16)
    chunk_main(1, pre)
    pre = chunk_pre(2)
    k_ref[0] = _dot(h, wc_ref[:, 0:D_ATT]).astype(BF16)
    chunk_main(2, pre)
    pre = chunk_pre(3)

    hi, mid, lo = _split3(nlf)
    run = carry_ref[...]
    blocks = []
    for c in range(TM // SSD_CHUNK):
        rows_c = slice(c * SSD_CHUNK, (c + 1) * SSD_CHUNK)
        blk = _dot(tri, hi[rows_c]) + _dot(tri, mid[rows_c]) + _dot(tri, lo[rows_c]) + run
        run = blk[SSD_CHUNK - 1:SSD_CHUNK, :]
        blocks.append(blk)
    cum = jnp.concatenate(blocks, axis=0)
    carry_ref[...] = run
    chunk_main(3, pre)
    hi, mid, lo = _split3(cum)
    lane_f = lax.broadcasted_iota(jnp.int32, cum.shape, 1) // 8
    parts = jnp.where(lane_f == F_HI // 8, hi.astype(F32),
                      jnp.where(lane_f == F_MID // 8, mid.astype(F32),
                                jnp.where(lane_f == F_LO // 8, lo.astype(F32), 0.0)))
    fp_ref[0] = parts.astype(BF16)


def _layer_spec(shape, l):
    zeros = (0,) * len(shape)
    return pl.BlockSpec((None,) + tuple(shape), lambda b, i: (l,) + zeros)


def _mod_spec(l):
    return pl.BlockSpec((None, 1, N_MOD, D_MODEL), lambda b, i: (l, b, 0, 0))


def _mixer_in(l, x, mod, g, w_rows, w_cols, bias_small, conv_w, conv_b, alog_row, dskip_row, gnorm_row, expand):
    bsz, s, d = x.shape
    ns = s // TM
    ncol = w_cols.shape[-1]
    out_shape = (
        jax.ShapeDtypeStruct((bsz, D_ATT, s), BF16),
        jax.ShapeDtypeStruct((bsz, ns, D_ATT, TM), BF16),
        jax.ShapeDtypeStruct((bsz, s, D_ATT), BF16),
        jax.ShapeDtypeStruct((bsz, s, LANES), BF16),
        jax.ShapeDtypeStruct((bsz, s, D_SSD), BF16),
    )
    return pl.pallas_call(
        _mixer_in_kernel,
        grid=(bsz, ns),
        in_specs=[pl.BlockSpec((1, TM, d), lambda b, i: (b, i, 0)),
                  _mod_spec(l),
                  _layer_spec((1, d), l),
                  _layer_spec((2 * D_ATT, d), l),
                  _layer_spec((d, ncol), l),
                  _layer_spec((1, LANES), l),
                  _layer_spec((SSD_CONV, D_XBC), l),
                  _layer_spec((1, D_XBC), l),
                  _layer_spec((1, LANES), l),
                  _layer_spec((1, D_SSD), l),
                  _layer_spec((1, D_SSD), l),
                  pl.BlockSpec((LANES, D_SSD), lambda b, i: (0, 0))],
        out_specs=(pl.BlockSpec((1, D_ATT, TM), lambda b, i: (b, 0, i)),
                   pl.BlockSpec((1, 1, D_ATT, TM), lambda b, i: (b, i, 0, 0)),
                   pl.BlockSpec((1, TM, D_ATT), lambda b, i: (b, i, 0)),
                   pl.BlockSpec((1, TM, LANES), lambda b, i: (b, i, 0)),
                   pl.BlockSpec((1, TM, D_SSD), lambda b, i: (b, i, 0))),
        out_shape=out_shape,
        scratch_shapes=[pltpu.VMEM((1, LANES), F32),
                        pltpu.VMEM((BF16_ROWS + TM, D_XBC), F32),
                        pltpu.VMEM((SSD_STATE, D_SSD), F32)],
        compiler_params=pltpu.CompilerParams(
            dimension_semantics=("parallel", "arbitrary"), vmem_limit_bytes=VMEM_LIMIT),
        name="mixer_in",
    )(x, mod, g, w_rows, w_cols, bias_small, conv_w, conv_b, alog_row, dskip_row, gnorm_row, expand)


def _attn_kernel(qT_ref, k_ref, f_ref, vT_ref, o_ref, s0_ref, s1_ref, acc_ref):
    h = pl.program_id(1)
    odd = h % 2
    nq = qT_ref.shape[2] // TQ
    r = lax.broadcasted_iota(jnp.int32, (LANES, TQ), 0)
    sel = jnp.where(r == h + F_HI, 1.0, jnp.where(r == h + F_MID, 1.0, jnp.where(r == h + F_LO, 1.0, 0.0)))
    sel = sel.astype(BF16)
    ones_rows = jnp.ones((BF16_ROWS, TK), BF16)
    bufs = (s0_ref, s1_ref)

    def make_qa(i):
        q = qT_ref[0, :, i * TQ:(i + 1) * TQ]
        zq = jnp.zeros_like(q)
        return jnp.concatenate([jnp.where(odd == 0, q, zq), jnp.where(odd == 1, q, zq), sel], axis=0)

    hk = TK // 2

    def logits(qa, i, t, dst):
        ka = jnp.concatenate([k_ref[0, t * TK:(t + 1) * TK, :], f_ref[0, t * TK:(t + 1) * TK, :]], axis=1)
        if t < i:
            dst[...] = _dot(ka, qa)
        else:
            dst[0:hk, :] = _dot(ka[0:hk], qa)
            dst[hk:, hk:] = _dot(ka[hk:], qa[:, hk:])

    def below_diagonal(rows, cols):
        key = lax.broadcasted_iota(jnp.int32, (rows, cols), 0)
        qry = lax.broadcasted_iota(jnp.int32, (rows, cols), 1)
        return key <= qry

    steps = [(i, t) for i in range(nq) for t in range(i + 1)]
    qa = make_qa(0)
    logits(qa, 0, 0, bufs[0])
    m = None
    for n, (i, t) in enumerate(steps):
        if n + 1 < len(steps):
            i2, t2 = steps[n + 1]
            qa_next = qa if i2 == i else make_qa(i2)
            logits(qa_next, i2, t2, bufs[(n + 1) % 2])
        va = jnp.concatenate([vT_ref[0, t], ones_rows], axis=0)
        if t < i:
            st = bufs[n % 2][...]
            mx = jnp.max(st, axis=0, keepdims=True)
            m_new = mx if t == 0 else jnp.maximum(m, mx)
            pv = _dot(va, jnp.exp2((st - m_new).astype(BF16)))
        else:
            top = jnp.where(below_diagonal(hk, TQ), bufs[n % 2][0:hk, :], NEG)
            bot = jnp.where(below_diagonal(hk, hk), bufs[n % 2][hk:, hk:], NEG)
            mx = jnp.max(top, axis=0, keepdims=True)
            mx = jnp.concatenate(
                [mx[:, :hk], jnp.maximum(mx[:, hk:], jnp.max(bot, axis=0, keepdims=True))], axis=1)
            m_new = mx if t == 0 else jnp.maximum(m, mx)
            pv = _dot(va[:, 0:hk], jnp.exp2((top - m_new).astype(BF16)))
            pv_bot = _dot(va[:, hk:], jnp.exp2((bot - m_new[:, hk:]).astype(BF16)))
            pv = jnp.concatenate([pv[:, :hk], pv[:, hk:] + pv_bot], axis=1)
        acc = pv if t == 0 else jnp.exp2(m - m_new) * acc_ref[...] + pv
        if t == i:
            o_ref[0, :, i * TQ:(i + 1) * TQ] = acc[:HEAD_DIM] * (1.0 / acc[HEAD_DIM:HEAD_DIM + 1])
        else:
            acc_ref[...] = acc
        m = m_new
        if n + 1 < len(steps):
            qa = qa_next


def _attention(qT, k, fparts, vT):
    bsz, _, s = qT.shape
    nk = s // TK
    return pl.pallas_call(
        _attn_kernel,
        grid=(bsz, N_HEADS),
        in_specs=[pl.BlockSpec((1, HEAD_DIM, s), lambda b, h: (b, h, 0)),
                  pl.BlockSpec((1, s, LANES), lambda b, h: (b, 0, h // 2)),
                  pl.BlockSpec((1, s, LANES), lambda b, h: (b, 0, 0)),
                  pl.BlockSpec((1, nk, HEAD_DIM, TK), lambda b, h: (b, 0, h, 0))],
        out_specs=pl.BlockSpec((1, HEAD_DIM, s), lambda b, h: (b, h, 0)),
        out_shape=jax.ShapeDtypeStruct((bsz, D_ATT, s), F32),
        scratch_shapes=[pltpu.VMEM((TK, TQ), F32), pltpu.VMEM((TK, TQ), F32),
                        pltpu.VMEM((HEAD_DIM + BF16_ROWS, TQ), F32)],
        compiler_params=pltpu.CompilerParams(
            dimension_semantics=("parallel", "parallel"), vmem_limit_bytes=VMEM_LIMIT),
        name="attention",
    )(qT, k, fparts, vT)


def _mlp_kernel(yT_ref, ys_ref, x_ref, mod_ref, ga_ref, wo_ref, gf_ref, wu_ref, cw_ref, cb_ref, wd_ref,
                gfin_ref, o_ref, halo_ref, ga0_ref, va0_ref, ga1_ref, va1_ref, act_ref, *, final_norm):
    ya = _rms(yT_ref[0].T) * ga_ref[...]
    ycat = jnp.concatenate([ya.astype(BF16), ys_ref[0]], axis=1)
    x1 = x_ref[0] + mod_ref[0, 2:3, :] * _dot(ycat, wo_ref[...])
    h2 = _rms(x1) * gf_ref[...]
    h2 = (h2 * (1.0 + mod_ref[0, 4:5, :]) + mod_ref[0, 3:4, :]).astype(BF16)

    @pl.when(pl.program_id(1) == 0)
    def _():
        halo_ref[...] = jnp.zeros_like(halo_ref)

    hext = jnp.concatenate([halo_ref[...], h2], axis=0)
    halo_ref[...] = h2[TM - BF16_ROWS:, :]

    first = BF16_ROWS - FFN_CONV + 1
    bufs = ((ga0_ref, va0_ref), (ga1_ref, va1_ref))
    n_chunks = D_FF // TF

    def up(j):
        bufg, bufv = bufs[j % 2]
        bufg[...] = _dot(hext, wu_ref[:, j * TF:(j + 1) * TF])
        bufv[...] = _dot(hext, wu_ref[:, D_FF + j * TF:D_FF + (j + 1) * TF])

    up(0)
    acc = None
    for j in range(n_chunks):
        if j + 1 < n_chunks:
            up(j + 1)
        bufg, bufv = bufs[j % 2]
        cols_g = slice(j * TF, (j + 1) * TF)
        cols_v = slice(D_FF + j * TF, D_FF + (j + 1) * TF)
        last = FFN_CONV - 1
        ug = bufg[BF16_ROWS:BF16_ROWS + TM, :] * cw_ref[last:last + 1, cols_g] + cb_ref[:, cols_g]
        uv = bufv[BF16_ROWS:BF16_ROWS + TM, :] * cw_ref[last:last + 1, cols_v] + cb_ref[:, cols_v]
        for kk in range(last):
            ug = ug + bufg[pl.ds(first + kk, TM), :] * cw_ref[kk:kk + 1, cols_g]
            uv = uv + bufv[pl.ds(first + kk, TM), :] * cw_ref[kk:kk + 1, cols_v]
        act_ref[:, cols_g] = (ug * _sigmoid(ug) * uv).astype(BF16)
        if (j + 1) % DOWN_GROUP == 0 or j + 1 == n_chunks:
            lo = (j // DOWN_GROUP) * DOWN_GROUP * TF
            part = _dot(act_ref[:, lo:(j + 1) * TF], wd_ref[lo:(j + 1) * TF, :])
            acc = part if acc is None else acc + part
    x2 = x1 + mod_ref[0, 5:6, :] * acc
    if final_norm:
        x2 = _rms(x2) * gfin_ref[...]
    o_ref[0] = x2


def _mlp(l, yT, yssd, x, mod, g_att, w_out, g_ffn, w_up, conv_w, conv_b, w_down, g_final, final_norm):
    bsz, s, d = x.shape
    resident = dict(pipeline_mode=pl.Buffered(1))
    conv_buf = pltpu.VMEM((BF16_ROWS + TM, TF), F32)

    def weight_spec(shape):
        return pl.BlockSpec((None,) + shape, lambda b, i: (l, 0, 0), **resident)

    return pl.pallas_call(
        functools.partial(_mlp_kernel, final_norm=final_norm),
        grid=(bsz, s // TM),
        in_specs=[pl.BlockSpec((1, D_ATT, TM), lambda b, i: (b, 0, i)),
                  pl.BlockSpec((1, TM, D_SSD), lambda b, i: (b, i, 0)),
                  pl.BlockSpec((1, TM, d), lambda b, i: (b, i, 0)),
                  _mod_spec(l),
                  _layer_spec((1, D_ATT), l),
                  weight_spec((D_ATT + D_SSD, d)),
                  _layer_spec((1, d), l),
                  weight_spec((d, 2 * D_FF)),
                  _layer_spec((FFN_CONV, 2 * D_FF), l),
                  _layer_spec((1, 2 * D_FF), l),
                  weight_spec((D_FF, d)),
                  pl.BlockSpec((1, d), lambda b, i: (0, 0))],
        out_specs=pl.BlockSpec((1, TM, d), lambda b, i: (b, i, 0)),
        out_shape=jax.ShapeDtypeStruct((bsz, s, d), F32),
        scratch_shapes=[pltpu.VMEM((BF16_ROWS, d), BF16), conv_buf, conv_buf, conv_buf, conv_buf,
                        pltpu.VMEM((TM, D_FF), BF16)],
        compiler_params=pltpu.CompilerParams(
            dimension_semantics=("parallel", "arbitrary"), vmem_limit_bytes=VMEM_LIMIT),
        name="mlp",
    )(yT, yssd, x, mod, g_att, w_out, g_ffn, w_up, conv_w, conv_b, w_down, g_final)


def _expand_matrix():
    e = np.zeros((LANES, D_SSD), np.float32)
    for hd in range(N_HEADS):
        e[DT_LO + hd, hd * HEAD_DIM:(hd + 1) * HEAD_DIM] = 1.0
    return jnp.asarray(e, BF16)


def _small_rows(f_vals, dt_vals):
    depth = f_vals.shape[0]
    pad = jnp.zeros((depth, LANES - F_LO - N_HEADS), F32)
    return jnp.concatenate([f_vals, dt_vals, f_vals, f_vals, pad], axis=1)[:, None, :]


def kernel(x, c, mod_w, mod_b, norm_mix_g, norm_ffn_g, w_in, fox_forget_b, attn_norm_g, ssd_conv_w, ssd_conv_b, ssd_dt_bias, ssd_a_log, ssd_d, ssd_norm_g, w_out, ffn_w_up, ffn_conv_w, ffn_conv_b, ffn_w_down, final_g):
    depth = w_in.shape[0]
    bsz = x.shape[0]
    mod = _modulation(c, mod_w, mod_b).reshape(depth, bsz, N_MOD, D_MODEL)
    expand = _expand_matrix()

    o_k, o_v, o_f = D_ATT, 2 * D_ATT, 3 * D_ATT
    o_z = o_f + N_HEADS
    o_x = o_z + D_SSD
    o_dt = o_x + D_XBC
    w_f = w_in[:, :, o_f:o_z]
    w_dt = w_in[:, :, o_dt:o_dt + N_HEADS]
    pad = jnp.zeros((depth, D_MODEL, LANES - F_LO - N_HEADS), F32)
    w_rows = jnp.concatenate([w_in[:, :, :o_k], w_in[:, :, o_v:o_f]], axis=2).transpose(0, 2, 1).astype(BF16)
    w_cols = jnp.concatenate([w_in[:, :, o_k:o_v], w_in[:, :, o_z:o_x], w_in[:, :, o_x:o_dt],
                              w_f, w_dt, w_f, w_f, pad], axis=2).astype(BF16)
    bias_small = _small_rows(fox_forget_b, ssd_dt_bias)
    alog_rows = _small_rows(jnp.zeros_like(ssd_a_log), ssd_a_log)
    dskip_rows = jnp.repeat(ssd_d, HEAD_DIM, axis=1)[:, None, :]
    row = lambda a: a[:, None, :]
    w_out_b = w_out.astype(BF16)
    w_up_b = ffn_w_up.astype(BF16)
    w_down_b = ffn_w_down.astype(BF16)
    g_final = final_g.reshape(1, -1)

    for l in range(depth):
        qT, vT, k, fparts, yssd = _mixer_in(
            l, x, mod, row(norm_mix_g), w_rows, w_cols, bias_small, ssd_conv_w, row(ssd_conv_b),
            alog_rows, dskip_rows, row(ssd_norm_g), expand)
        yT = _attention(qT, k, fparts, vT)
        x = _mlp(l, yT, yssd, x, mod, row(attn_norm_g), w_out_b, row(norm_ffn_g), w_up_b, ffn_conv_w,
                 row(ffn_conv_b), w_down_b, g_final, final_norm=(l == depth - 1))
    return x
```

```python
import functools

import numpy as np
import jax
import jax.numpy as jnp
from jax import lax
from jax.experimental import pallas as pl
from jax.experimental.pallas import tpu as pltpu

F32 = jnp.float32
BF16 = jnp.bfloat16

D_MODEL = 1024
N_HEADS = 8
HEAD_DIM = 64
D_ATT = 512
D_SSD = 512
SSD_GROUPS = 2
SSD_STATE = 128
SSD_CONV = 4
SSD_CHUNK = 128
D_XBC = D_SSD + 2 * SSD_GROUPS * SSD_STATE
D_FF = 2816
FFN_CONV = 3
N_MOD = 6
EPS = 1e-6
NEG = -1e30
LOG2E = 1.4426950408889634

LANES = 128
BF16_ROWS = 16
VMEM_LIMIT = 56 * 1024 * 1024

TM = 512
TQ = 512
TK = 512
TF = 256
DOWN_GROUP = 6

F_HI, DT_LO, F_MID, F_LO = 0, 8, 16, 24


def _sigmoid(x):
    return 1.0 / (1.0 + jnp.exp(-x))


def _split2(x):
    hi = x.astype(BF16)
    lo = (x - hi.astype(F32)).astype(BF16)
    return hi, lo


def _split3(x):
    hi = x.astype(BF16)
    r = x - hi.astype(F32)
    mid = r.astype(BF16)
    lo = (r - mid.astype(F32)).astype(BF16)
    return hi, mid, lo


def _dot(a, b):
    return jnp.dot(a, b, preferred_element_type=F32)


def _dot_nt(a, b):
    return lax.dot_general(a, b, (((1,), (1,)), ((), ())), preferred_element_type=F32)


def _tril(n):
    r = lax.broadcasted_iota(jnp.int32, (n, n), 0)
    c = lax.broadcasted_iota(jnp.int32, (n, n), 1)
    return jnp.where(c <= r, 1.0, 0.0).astype(BF16)


def _rms(x):
    return x * lax.rsqrt(jnp.mean(x * x, axis=-1, keepdims=True) + EPS)


def _mod_kernel(c_ref, w_ref, b_ref, o_ref):
    c = c_ref[...]
    o_ref[0] = _dot(c * _sigmoid(c), w_ref[0]) + b_ref[0]


def _modulation(c, mod_w, mod_b):
    depth, d, n = mod_w.shape
    bsz = c.shape[0]
    tn = 1536
    return pl.pallas_call(
        _mod_kernel,
        grid=(depth, n // tn),
        in_specs=[pl.BlockSpec((bsz, d), lambda l, j: (0, 0)),
                  pl.BlockSpec((1, d, tn), lambda l, j: (l, 0, j)),
                  pl.BlockSpec((1, 1, tn), lambda l, j: (l, 0, j))],
        out_specs=pl.BlockSpec((1, bsz, tn), lambda l, j: (l, 0, j)),
        out_shape=jax.ShapeDtypeStruct((depth, bsz, n), F32),
        compiler_params=pltpu.CompilerParams(
            dimension_semantics=("parallel", "parallel"), vmem_limit_bytes=VMEM_LIMIT),
        name="modulation",
    )(c, mod_w, mod_b.reshape(depth, 1, n))


def _mixer_in_kernel(x_ref, mod_ref, g_ref, wr_ref, wc_ref, bs_ref, cw_ref, cb_ref, alog_ref, dskip_ref,
                     gn_ref, e_ref, qT_ref, vT_ref, k_ref, fp_ref, y_ref, carry_ref, buf_ref, state_ref):
    @pl.when(pl.program_id(1) == 0)
    def _():
        carry_ref[...] = jnp.zeros_like(carry_ref)
        state_ref[...] = jnp.zeros_like(state_ref)
        buf_ref[0:BF16_ROWS, :] = jnp.zeros((BF16_ROWS, D_XBC), F32)

    x = x_ref[0]
    y = _rms(x) * g_ref[...]
    h = (y * (1.0 + mod_ref[0, 1:2, :]) + mod_ref[0, 0:1, :]).astype(BF16)

    o = D_ATT + D_SSD
    buf_ref[BF16_ROWS:BF16_ROWS + TM, :] = _dot(h, wc_ref[:, o:o + D_XBC])
    small = _dot(h, wc_ref[:, o + D_XBC:o + D_XBC + LANES]) + bs_ref[...]

    z = _dot(h, wc_ref[:, D_ATT:D_ATT + D_SSD])

    conv = buf_ref[BF16_ROWS:BF16_ROWS + TM, :] * cw_ref[SSD_CONV - 1:SSD_CONV, :] + cb_ref[...]
    for kk in range(SSD_CONV - 1):
        conv = conv + buf_ref[pl.ds(BF16_ROWS - SSD_CONV + 1 + kk, TM), :] * cw_ref[kk:kk + 1, :]
    buf_ref[0:BF16_ROWS, :] = buf_ref[TM:TM + BF16_ROWS, :]
    xc = conv * _sigmoid(conv)
    xs = xc[:, :D_SSD]
    gn = SSD_GROUPS * SSD_STATE
    bm = xc[:, D_SSD:D_SSD + gn]
    cm = xc[:, D_SSD + gn:]

    t = jnp.log1p(jnp.exp(-jnp.abs(small)))
    dt_all = jnp.maximum(small, 0.0) + t
    nlf = (jnp.maximum(-small, 0.0) + t) * LOG2E

    lane = lax.broadcasted_iota(jnp.int32, (1, LANES), 1) // 8
    head_lanes = lane == DT_LO // 8
    a_neg = jnp.where(head_lanes, -jnp.exp(alog_ref[...]), 0.0)
    dt_small = jnp.where(head_lanes, dt_all, 0.0)
    a_small = dt_small * a_neg
    e = e_ref[...]
    dhi, dlo = _split2(dt_small)
    dt_exp = _dot(dhi, e) + _dot(dlo, e)

    tri = _tril(SSD_CHUNK)
    rr = lax.broadcasted_iota(jnp.int32, (SSD_CHUNK, SSD_CHUNK), 0)
    cc = lax.broadcasted_iota(jnp.int32, (SSD_CHUNK, SSD_CHUNK), 1)
    causal = cc <= rr
    lane2 = lax.broadcasted_iota(jnp.int32, (SSD_CHUNK, LANES), 1)
    heads_per_group = N_HEADS // SSD_GROUPS
    gw = heads_per_group * HEAD_DIM

    def chunk_pre(c):
        rows_c = slice(c * SSD_CHUNK, (c + 1) * SSD_CHUNK)
        ahi, alo = _split2(a_small[rows_c])
        cs = _dot(tri, ahi) + _dot(tri, alo)
        chi, clo = _split2(cs)
        cs_exp = _dot(chi, e) + _dot(clo, e)
        cs_t = cs.T
        cbs, y_offs = [], []
        for g in range(SSD_GROUPS):
            bg = bm[rows_c, g * SSD_STATE:(g + 1) * SSD_STATE]
            cg = cm[rows_c, g * SSD_STATE:(g + 1) * SSD_STATE].astype(BF16)
            cbs.append(_dot_nt(cg, bg.astype(BF16)))
            y_offs.append(_dot(cg, state_ref[:, g * gw:(g + 1) * gw].astype(BF16)))
        return cs, cs_exp, cs_t, cbs, y_offs

    def chunk_main(c, pre):
        cs, cs_exp, cs_t, cbs, y_offs = pre
        rows_c = slice(c * SSD_CHUNK, (c + 1) * SSD_CHUNK)
        last = cs_exp[SSD_CHUNK - 1:SSD_CHUNK, :]
        dec_end = jnp.exp(last - cs_exp)
        dec_start = jnp.exp(cs_exp)
        dec_chunk = jnp.exp(last)
        xs_c = xs[rows_c]
        xdt = xs_c * dt_exp[rows_c]
        xde = (xdt * dec_end).astype(BF16)
        outs = []
        for g in range(SSD_GROUPS):
            bg = bm[rows_c, g * SSD_STATE:(g + 1) * SSD_STATE]
            cb = cbs[g]
            y_off = y_offs[g] * dec_start[:, g * gw:(g + 1) * gw]
            pair_out = []
            for pr in range(heads_per_group // 2):
                lo_l = g * gw + pr * LANES
                xp = xdt[:, lo_l:lo_l + LANES]
                acc = None
                for half in range(2):
                    hd = g * heads_per_group + pr * 2 + half
                    col = cs[:, DT_LO + hd:DT_LO + hd + 1]
                    row = cs_t[DT_LO + hd:DT_LO + hd + 1, :]
                    seg = jnp.where(causal, col - row, NEG)
                    mmat = (cb * jnp.exp(seg)).astype(BF16)
                    keep_half = (lane2 < HEAD_DIM) if half == 0 else (lane2 >= HEAD_DIM)
                    xh = jnp.where(keep_half, xp, 0.0).astype(BF16)
                    term = _dot(mmat, xh)
                    acc = term if acc is None else acc + term
                pair_out.append(acc)
            y_diag = jnp.concatenate(pair_out, axis=1)
            outs.append(y_diag + y_off)
            st = state_ref[:, g * gw:(g + 1) * gw]
            new_state = st * dec_chunk[:, g * gw:(g + 1) * gw] + _dot(bg.T.astype(BF16), xde[:, g * gw:(g + 1) * gw])
            state_ref[:, g * gw:(g + 1) * gw] = new_state
        yc = jnp.concatenate(outs, axis=1) + xs_c * dskip_ref[...]
        zc = z[rows_c]
        yc = yc * (zc * _sigmoid(zc))
        normed = [_rms(yc[:, g * gw:(g + 1) * gw]) for g in range(SSD_GROUPS)]
        y_ref[0, rows_c, :] = (jnp.concatenate(normed, axis=1) * gn_ref[...]).astype(BF16)

    pre = chunk_pre(0)
    qT_ref[0] = (_dot_nt(wr_ref[0:D_ATT, :], h) * (LOG2E * HEAD_DIM ** -0.5)).astype(BF16)
    chunk_main(0, pre)
    pre = chunk_pre(1)
    vT_ref[0, 0] = _dot_nt(wr_ref[D_ATT:2 * D_ATT, :], h).astype(BF16)
    chunk_main(1, pre)
    pre = chunk_pre(2)
    k_ref[0] = _dot(h, wc_ref[:, 0:D_ATT]).astype(BF16)
    chunk_main(2, pre)
    pre = chunk_pre(3)

    hi, mid, lo = _split3(nlf)
    run = carry_ref[...]
    blocks = []
    for c in range(TM // SSD_CHUNK):
        rows_c = slice(c * SSD_CHUNK, (c + 1) * SSD_CHUNK)
        blk = _dot(tri, hi[rows_c]) + _dot(tri, mid[rows_c]) + _dot(tri, lo[rows_c]) + run
        run = blk[SSD_CHUNK - 1:SSD_CHUNK, :]
        blocks.append(blk)
    cum = jnp.concatenate(blocks, axis=0)
    carry_ref[...] = run
    chunk_main(3, pre)
    hi, mid, lo = _split3(cum)
    lane_f = lax.broadcasted_iota(jnp.int32, cum.shape, 1) // 8
    parts = jnp.where(lane_f == F_HI // 8, hi.astype(F32),
                      jnp.where(lane_f == F_MID // 8, mid.astype(F32),
                                jnp.where(lane_f == F_LO // 8, lo.astype(F32), 0.0)))
    fp_ref[0] = parts.astype(BF16)


def _layer_spec(shape, l):
    zeros = (0,) * len(shape)
    return pl.BlockSpec((None,) + tuple(shape), lambda b, i: (l,) + zeros)


def _mod_spec(l):
    return pl.BlockSpec((None, 1, N_MOD, D_MODEL), lambda b, i: (l, b, 0, 0))


def _mixer_in(l, x, mod, g, w_rows, w_cols, bias_small, conv_w, conv_b, alog_row, dskip_row, gnorm_row, expand):
    bsz, s, d = x.shape
    ns = s // TM
    ncol = w_cols.shape[-1]
    out_shape = (
        jax.ShapeDtypeStruct((bsz, D_ATT, s), BF16),
        jax.ShapeDtypeStruct((bsz, ns, D_ATT, TM), BF16),
        jax.ShapeDtypeStruct((bsz, s, D_ATT), BF16),
        jax.ShapeDtypeStruct((bsz, s, LANES), BF16),
        jax.ShapeDtypeStruct((bsz, s, D_SSD), BF16),
    )
    return pl.pallas_call(
        _mixer_in_kernel,
        grid=(bsz, ns),
        in_specs=[pl.BlockSpec((1, TM, d), lambda b, i: (b, i, 0)),
                  _mod_spec(l),
                  _layer_spec((1, d), l),
                  _layer_spec((2 * D_ATT, d), l),
                  _layer_spec((d, ncol), l),
                  _layer_spec((1, LANES), l),
                  _layer_spec((SSD_CONV, D_XBC), l),
                  _layer_spec((1, D_XBC), l),
                  _layer_spec((1, LANES), l),
                  _layer_spec((1, D_SSD), l),
                  _layer_spec((1, D_SSD), l),
                  pl.BlockSpec((LANES, D_SSD), lambda b, i: (0, 0))],
        out_specs=(pl.BlockSpec((1, D_ATT, TM), lambda b, i: (b, 0, i)),
                   pl.BlockSpec((1, 1, D_ATT, TM), lambda b, i: (b, i, 0, 0)),
                   pl.BlockSpec((1, TM, D_ATT), lambda b, i: (b, i, 0)),
                   pl.BlockSpec((1, TM, LANES), lambda b, i: (b, i, 0)),
                   pl.BlockSpec((1, TM, D_SSD), lambda b, i: (b, i, 0))),
        out_shape=out_shape,
        scratch_shapes=[pltpu.VMEM((1, LANES), F32),
                        pltpu.VMEM((BF16_ROWS + TM, D_XBC), F32),
                        pltpu.VMEM((SSD_STATE, D_SSD), F32)],
        compiler_params=pltpu.CompilerParams(
            dimension_semantics=("parallel", "arbitrary"), vmem_limit_bytes=VMEM_LIMIT),
        name="mixer_in",
    )(x, mod, g, w_rows, w_cols, bias_small, conv_w, conv_b, alog_row, dskip_row, gnorm_row, expand)


def _attn_kernel(qT_ref, k_ref, f_ref, vT_ref, o_ref, s0_ref, s1_ref, acc_ref):
    h = pl.program_id(1)
    odd = h % 2
    nq = qT_ref.shape[2] // TQ
    r = lax.broadcasted_iota(jnp.int32, (LANES, TQ), 0)
    sel = jnp.where(r == h + F_HI, 1.0, jnp.where(r == h + F_MID, 1.0, jnp.where(r == h + F_LO, 1.0, 0.0)))
    sel = sel.astype(BF16)
    ones_rows = jnp.ones((BF16_ROWS, TK), BF16)
    bufs = (s0_ref, s1_ref)

    def make_qa(i):
        q = qT_ref[0, :, i * TQ:(i + 1) * TQ]
        zq = jnp.zeros_like(q)
        return jnp.concatenate([jnp.where(odd == 0, q, zq), jnp.where(odd == 1, q, zq), sel], axis=0)

    hk = TK // 2

    def logits(qa, i, t, dst):
        ka = jnp.concatenate([k_ref[0, t * TK:(t + 1) * TK, :], f_ref[0, t * TK:(t + 1) * TK, :]], axis=1)
        if t < i:
            dst[...] = _dot(ka, qa)
        else:
            dst[0:hk, :] = _dot(ka[0:hk], qa)
            dst[hk:, hk:] = _dot(ka[hk:], qa[:, hk:])

    def below_diagonal(rows, cols):
        key = lax.broadcasted_iota(jnp.int32, (rows, cols), 0)
        qry = lax.broadcasted_iota(jnp.int32, (rows, cols), 1)
        return key <= qry

    steps = [(i, t) for i in range(nq) for t in range(i + 1)]
    qa = make_qa(0)
    logits(qa, 0, 0, bufs[0])
    m = None
    for n, (i, t) in enumerate(steps):
        if n + 1 < len(steps):
            i2, t2 = steps[n + 1]
            qa_next = qa if i2 == i else make_qa(i2)
            logits(qa_next, i2, t2, bufs[(n + 1) % 2])
        va = jnp.concatenate([vT_ref[0, t], ones_rows], axis=0)
        if t < i:
            st = bufs[n % 2][...]
            mx = jnp.max(st, axis=0, keepdims=True)
            m_new = mx if t == 0 else jnp.maximum(m, mx)
            pv = _dot(va, jnp.exp2((st - m_new).astype(BF16)))
        else:
            top = jnp.where(below_diagonal(hk, TQ), bufs[n % 2][0:hk, :], NEG)
            bot = jnp.where(below_diagonal(hk, hk), bufs[n % 2][hk:, hk:], NEG)
            mx = jnp.max(top, axis=0, keepdims=True)
            mx = jnp.concatenate(
                [mx[:, :hk], jnp.maximum(mx[:, hk:], jnp.max(bot, axis=0, keepdims=True))], axis=1)
            m_new = mx if t == 0 else jnp.maximum(m, mx)
            pv = _dot(va[:, 0:hk], jnp.exp2((top - m_new).astype(BF16)))
            pv_bot = _dot(va[:, hk:], jnp.exp2((bot - m_new[:, hk:]).astype(BF16)))
            pv = jnp.concatenate([pv[:, :hk], pv[:, hk:] + pv_bot], axis=1)
        acc = pv if t == 0 else jnp.exp2(m - m_new) * acc_ref[...] + pv
        if t == i:
            o_ref[0, :, i * TQ:(i + 1) * TQ] = acc[:HEAD_DIM] * (1.0 / acc[HEAD_DIM:HEAD_DIM + 1])
        else:
            acc_ref[...] = acc
        m = m_new
        if n + 1 < len(steps):
            qa = qa_next


def _attention(qT, k, fparts, vT):
    bsz, _, s = qT.shape
    nk = s // TK
    return pl.pallas_call(
        _attn_kernel,
        grid=(bsz, N_HEADS),
        in_specs=[pl.BlockSpec((1, HEAD_DIM, s), lambda b, h: (b, h, 0)),
                  pl.BlockSpec((1, s, LANES), lambda b, h: (b, 0, h // 2)),
                  pl.BlockSpec((1, s, LANES), lambda b, h: (b, 0, 0)),
                  pl.BlockSpec((1, nk, HEAD_DIM, TK), lambda b, h: (b, 0, h, 0))],
        out_specs=pl.BlockSpec((1, HEAD_DIM, s), lambda b, h: (b, h, 0)),
        out_shape=jax.ShapeDtypeStruct((bsz, D_ATT, s), F32),
        scratch_shapes=[pltpu.VMEM((TK, TQ), F32), pltpu.VMEM((TK, TQ), F32),
                        pltpu.VMEM((HEAD_DIM + BF16_ROWS, TQ), F32)],
        compiler_params=pltpu.CompilerParams(
            dimension_semantics=("parallel", "parallel"), vmem_limit_bytes=VMEM_LIMIT),
        name="attention",
    )(qT, k, fparts, vT)


def _mlp_kernel(yT_ref, ys_ref, x_ref, mod_ref, ga_ref, wo_ref, gf_ref, wu_ref, cw_ref, cb_ref, wd_ref,
                gfin_ref, o_ref, halo_ref, ga0_ref, va0_ref, ga1_ref, va1_ref, act_ref, *, final_norm):
    x1_parts, h2_parts = [], []
    for r0 in range(0, TM, TM // 2):
        rows = slice(r0, r0 + TM // 2)
        ya = _rms(yT_ref[0, :, rows].T) * ga_ref[...]
        ycat = jnp.concatenate([ya.astype(BF16), ys_ref[0, rows, :]], axis=1)
        x1h = x_ref[0, rows, :] + mod_ref[0, 2:3, :] * _dot(ycat, wo_ref[...])
        h2h = _rms(x1h) * gf_ref[...]
        h2_parts.append((h2h * (1.0 + mod_ref[0, 4:5, :]) + mod_ref[0, 3:4, :]).astype(BF16))
        x1_parts.append(x1h)
    x1 = jnp.concatenate(x1_parts, axis=0)
    h2 = jnp.concatenate(h2_parts, axis=0)

    @pl.when(pl.program_id(1) == 0)
    def _():
        halo_ref[...] = jnp.zeros_like(halo_ref)

    hext = jnp.concatenate([halo_ref[...], h2], axis=0)
    halo_ref[...] = h2[TM - BF16_ROWS:, :]

    first = BF16_ROWS - FFN_CONV + 1
    bufs = ((ga0_ref, va0_ref), (ga1_ref, va1_ref))
    n_chunks = D_FF // TF

    def up(j):
        bufg, bufv = bufs[j % 2]
        bufg[...] = _dot(hext, wu_ref[:, j * TF:(j + 1) * TF])
        bufv[...] = _dot(hext, wu_ref[:, D_FF + j * TF:D_FF + (j + 1) * TF])

    up(0)
    acc = None
    for j in range(n_chunks):
        if j + 1 < n_chunks:
            up(j + 1)
        bufg, bufv = bufs[j % 2]
        cols_g = slice(j * TF, (j + 1) * TF)
        cols_v = slice(D_FF + j * TF, D_FF + (j + 1) * TF)
        last = FFN_CONV - 1
        ug = bufg[BF16_ROWS:BF16_ROWS + TM, :] * cw_ref[last:last + 1, cols_g] + cb_ref[:, cols_g]
        uv = bufv[BF16_ROWS:BF16_ROWS + TM, :] * cw_ref[last:last + 1, cols_v] + cb_ref[:, cols_v]
        for kk in range(last):
            ug = ug + bufg[pl.ds(first + kk, TM), :] * cw_ref[kk:kk + 1, cols_g]
            uv = uv + bufv[pl.ds(first + kk, TM), :] * cw_ref[kk:kk + 1, cols_v]
        act_ref[:, cols_g] = (ug * _sigmoid(ug) * uv).astype(BF16)
        if (j + 1) % DOWN_GROUP == 0 or j + 1 == n_chunks:
            lo = (j // DOWN_GROUP) * DOWN_GROUP * TF
            part = _dot(act_ref[:, lo:(j + 1) * TF], wd_ref[lo:(j + 1) * TF, :])
            acc = part if acc is None else acc + part
    x2 = x1 + mod_ref[0, 5:6, :] * acc
    if final_norm:
        x2 = _rms(x2) * gfin_ref[...]
    o_ref[0] = x2


def _mlp(l, yT, yssd, x, mod, g_att, w_out, g_ffn, w_up, conv_w, conv_b, w_down, g_final, final_norm):
    bsz, s, d = x.shape
    resident = dict(pipeline_mode=pl.Buffered(1))
    conv_buf = pltpu.VMEM((BF16_ROWS + TM, TF), F32)

    def weight_spec(shape):
        return pl.BlockSpec((None,) + shape, lambda b, i: (l, 0, 0), **resident)

    return pl.pallas_call(
        functools.partial(_mlp_kernel, final_norm=final_norm),
        grid=(bsz, s // TM),
        in_specs=[pl.BlockSpec((1, D_ATT, TM), lambda b, i: (b, 0, i)),
                  pl.BlockSpec((1, TM, D_SSD), lambda b, i: (b, i, 0)),
                  pl.BlockSpec((1, TM, d), lambda b, i: (b, i, 0)),
                  _mod_spec(l),
                  _layer_spec((1, D_ATT), l),
                  weight_spec((D_ATT + D_SSD, d)),
                  _layer_spec((1, d), l),
                  weight_spec((d, 2 * D_FF)),
                  _layer_spec((FFN_CONV, 2 * D_FF), l),
                  _layer_spec((1, 2 * D_FF), l),
                  weight_spec((D_FF, d)),
                  pl.BlockSpec((1, d), lambda b, i: (0, 0))],
        out_specs=pl.BlockSpec((1, TM, d), lambda b, i: (b, i, 0)),
        out_shape=jax.ShapeDtypeStruct((bsz, s, d), F32),
        scratch_shapes=[pltpu.VMEM((BF16_ROWS, d), BF16), conv_buf, conv_buf, conv_buf, conv_buf,
                        pltpu.VMEM((TM, D_FF), BF16)],
        compiler_params=pltpu.CompilerParams(
            dimension_semantics=("parallel", "arbitrary"), vmem_limit_bytes=VMEM_LIMIT),
        name="mlp",
    )(yT, yssd, x, mod, g_att, w_out, g_ffn, w_up, conv_w, conv_b, w_down, g_final)


def _expand_matrix():
    e = np.zeros((LANES, D_SSD), np.float32)
    for hd in range(N_HEADS):
        e[DT_LO + hd, hd * HEAD_DIM:(hd + 1) * HEAD_DIM] = 1.0
    return jnp.asarray(e, BF16)


def _small_rows(f_vals, dt_vals):
    depth = f_vals.shape[0]
    pad = jnp.zeros((depth, LANES - F_LO - N_HEADS), F32)
    return jnp.concatenate([f_vals, dt_vals, f_vals, f_vals, pad], axis=1)[:, None, :]


def kernel(x, c, mod_w, mod_b, norm_mix_g, norm_ffn_g, w_in, fox_forget_b, attn_norm_g, ssd_conv_w, ssd_conv_b, ssd_dt_bias, ssd_a_log, ssd_d, ssd_norm_g, w_out, ffn_w_up, ffn_conv_w, ffn_conv_b, ffn_w_down, final_g):
    depth = w_in.shape[0]
    bsz = x.shape[0]
    mod = _modulation(c, mod_w, mod_b).reshape(depth, bsz, N_MOD, D_MODEL)
    expand = _expand_matrix()

    o_k, o_v, o_f = D_ATT, 2 * D_ATT, 3 * D_ATT
    o_z = o_f + N_HEADS
    o_x = o_z + D_SSD
    o_dt = o_x + D_XBC
    w_f = w_in[:, :, o_f:o_z]
    w_dt = w_in[:, :, o_dt:o_dt + N_HEADS]
    pad = jnp.zeros((depth, D_MODEL, LANES - F_LO - N_HEADS), F32)
    w_rows = jnp.concatenate([w_in[:, :, :o_k], w_in[:, :, o_v:o_f]], axis=2).transpose(0, 2, 1).astype(BF16)
    w_cols = jnp.concatenate([w_in[:, :, o_k:o_v], w_in[:, :, o_z:o_x], w_in[:, :, o_x:o_dt],
                              w_f, w_dt, w_f, w_f, pad], axis=2).astype(BF16)
    bias_small = _small_rows(fox_forget_b, ssd_dt_bias)
    alog_rows = _small_rows(jnp.zeros_like(ssd_a_log), ssd_a_log)
    dskip_rows = jnp.repeat(ssd_d, HEAD_DIM, axis=1)[:, None, :]
    row = lambda a: a[:, None, :]
    w_out_b = w_out.astype(BF16)
    w_up_b = ffn_w_up.astype(BF16)
    w_down_b = ffn_w_down.astype(BF16)
    g_final = final_g.reshape(1, -1)

    for l in range(depth):
        qT, vT, k, fparts, yssd = _mixer_in(
            l, x, mod, row(norm_mix_g), w_rows, w_cols, bias_small, ssd_conv_w, row(ssd_conv_b),
            alog_rows, dskip_rows, row(ssd_norm_g), expand)
        yT = _attention(qT, k, fparts, vT)
        x = _mlp(l, yT, yssd, x, mod, row(attn_norm_g), w_out_b, row(norm_ffn_g), w_up_b, ffn_conv_w,
                 row(ffn_conv_b), w_down_b, g_final, final_norm=(l == depth - 1))
    return x
```
